```python
import math
import jax, jax.numpy as jnp
from jax import lax
import numpy as np

D_MODEL = 1024
BATCH = 2
SEQ = 8192
DEPTH = 4
DEC_BATCH = 128
DEC_SEQ = 8
PAST_LEN = 2048
PAGE_SIZE = 128

N_A = DEPTH // 2
N_B = DEPTH - N_A
RET_HEADS = 8
RET_DK = D_MODEL // RET_HEADS
RET_DV = 2 * RET_DK
RET_QK_W = RET_HEADS * RET_DK
RET_V_W = RET_HEADS * RET_DV
RET_IN_W = 2 * RET_QK_W + 2 * RET_V_W
RET_CHUNK = 128
ROPE_BASE = 10000.0
SB_HEADS = 16
SB_DH = D_MODEL // SB_HEADS
SB_W = SB_HEADS * SB_DH
SB_IN_W = 2 * SB_W
SB_QBLOCK = 128
SB_BIAS_INIT = -6.0
EPS = 1e-6

kernel_name = "yoco_retention_stickbreaking_step"

F32 = jnp.float32


def rmsnorm(x, g):
    xf = x.astype(F32)
    return xf * lax.rsqrt(jnp.mean(xf * xf, axis=-1, keepdims=True) + EPS) * g.astype(F32)


def modulate(x, c, g, w, b):
    m = jax.nn.silu(c) @ w + b
    shift, scale, gate = jnp.split(m.astype(F32), 3, axis=-1)
    h = rmsnorm(x, g) * (1.0 + scale[:, None, :]) + shift[:, None, :]
    return h, gate[:, None, :]


def rotary(x, pos):
    half = x.shape[-1] // 2
    inv_freq = ROPE_BASE ** (-jnp.arange(half, dtype=F32) / half)
    ang = pos.astype(F32)[:, None] * inv_freq[None, :]
    cos = jnp.cos(ang)[None, :, None, :]
    sin = jnp.sin(ang)[None, :, None, :]
    x1, x2 = x[..., :half], x[..., half:]
    return jnp.concatenate([x1 * cos - x2 * sin, x1 * sin + x2 * cos], axis=-1)


def retention(q, k, v, s0):
    B, L, H, dk = q.shape
    dv = v.shape[-1]
    C = math.gcd(L, RET_CHUNK)
    n = L // C
    log_gamma = jnp.log1p(-jnp.exp2(-5.0 - jnp.arange(H, dtype=F32)))
    idx = jnp.arange(C, dtype=F32)
    diff = idx[:, None] - idx[None, :]
    decay_mask = jnp.where(diff[None] >= 0, jnp.exp(log_gamma[:, None, None] * jnp.maximum(diff, 0.0)[None]), 0.0)
    q_dec = jnp.exp(log_gamma[None, :] * (idx[:, None] + 1.0))[None, :, :, None]
    k_dec = jnp.exp(log_gamma[None, :] * (C - 1.0 - idx[:, None]))[None, :, :, None]
    chunk_dec = jnp.exp(log_gamma * C)[None, :, None, None]

    def chunks(t):
        return jnp.moveaxis(t.reshape(B, n, C, H, t.shape[-1]), 1, 0)

    def body(S, xs):
        qc, kc, vc = xs
        scores = jnp.einsum('bihd,bjhd->bhij', qc, kc) * decay_mask[None]
        intra = jnp.einsum('bhij,bjhe->bihe', scores, vc)
        cross = jnp.einsum('bihd,bhde->bihe', qc * q_dec, S)
        S_new = S * chunk_dec + jnp.einsum('bjhd,bjhe->bhde', kc * k_dec, vc)
        return S_new, intra + cross

    s_final, outs = lax.scan(body, s0, (chunks(q), chunks(k), chunks(v)))
    return jnp.moveaxis(outs, 0, 1).reshape(B, L, H, dv), s_final


def stick_breaking(q, k, v, q_pos, k_pos, bias):
    B, L, H, d = q.shape
    Qb = math.gcd(L, SB_QBLOCK)
    nb = L // Qb
    qb = jnp.moveaxis(q.reshape(B, nb, Qb, H, d), 1, 0)
    pb = q_pos.reshape(nb, Qb)
    scale = d ** -0.5
    bias_f = bias.astype(F32)[None, :, None, None]

    def block(args):
        qi, pi = args
        z = jnp.einsum('bqhd,bkhd->bhqk', qi, k).astype(F32) * scale + bias_f
        mask = (k_pos[None, :] < pi[:, None])[None, None]
        log_keep = jnp.where(mask, jax.nn.log_sigmoid(-z), 0.0)
        tail = lax.cumsum(log_keep, axis=3, reverse=True)
        tail_excl = jnp.concatenate([tail[..., 1:], jnp.zeros_like(tail[..., :1])], axis=-1)
        w = jnp.where(mask, jnp.exp(jax.nn.log_sigmoid(z) + tail_excl), 0.0)
        return jnp.einsum('bhqk,bkhd->bqhd', w, v)

    o = lax.map(block, (qb, pb))
    return jnp.moveaxis(o, 0, 1).reshape(B, L, H, d)


def retention_layer(h, pos, s0, w_in, norm_g, w_out):
    B, L, _ = h.shape
    proj = h @ w_in
    q, k, v, g = jnp.split(proj, [RET_QK_W, 2 * RET_QK_W, 2 * RET_QK_W + RET_V_W], axis=-1)
    q = rotary(q.reshape(B, L, RET_HEADS, RET_DK), pos)
    k = rotary(k.reshape(B, L, RET_HEADS, RET_DK) * (RET_DK ** -0.5), pos)
    v = v.reshape(B, L, RET_HEADS, RET_DV)
    o, s_new = retention(q, k, v, s0)
    o = rmsnorm(o, norm_g.reshape(RET_HEADS, RET_DV)).reshape(B, L, RET_V_W)
    return (jax.nn.silu(g) * o) @ w_out, s_new


def shared_kv(x, kv_norm_g, w_kv, sb_k_g):
    B, L, _ = x.shape
    kv = rmsnorm(x, kv_norm_g) @ w_kv
    k, v = jnp.split(kv, 2, axis=-1)
    k = rmsnorm(k.reshape(B, L, SB_HEADS, SB_DH), sb_k_g)
    return k, v.reshape(B, L, SB_HEADS, SB_DH)


def sb_layer(h, pos, k_all, v_all, k_pos, w_in, q_g, bias, w_out):
    B, L, _ = h.shape
    q, g = jnp.split(h @ w_in, 2, axis=-1)
    q = rmsnorm(q.reshape(B, L, SB_HEADS, SB_DH), q_g)
    o = stick_breaking(q, k_all, v_all, pos, k_pos, bias).reshape(B, L, SB_W)
    return (jax.nn.silu(g) * o) @ w_out


def trunk(x, c, pos, ret_state, past_k, past_v,
          ada_w, ada_b, norm_g, ret_w_in, ret_norm_g, ret_w_out,
          kv_norm_g, w_kv, sb_q_g, sb_k_g, sb_w_in, sb_w_out, sb_bias):
    x = x.astype(F32)
    c = c.astype(F32)
    B = x.shape[0]
    new_states = []
    for l in range(N_A):
        h, gate = modulate(x, c, norm_g[l], ada_w[l], ada_b[l])
        if ret_state is None:
            s0 = jnp.zeros((B, RET_HEADS, RET_DK, RET_DV), F32)
        else:
            s0 = ret_state[l].astype(F32)
        y, s_new = retention_layer(h, pos, s0, ret_w_in[l], ret_norm_g[l], ret_w_out[l])
        x = x + gate * y
        new_states.append(s_new)
    k_new, v_new = shared_kv(x, kv_norm_g, w_kv, sb_k_g)
    if past_k is None:
        k_all, v_all, k_pos = k_new, v_new, pos
    else:
        past_len = past_k.shape[1]
        k_all = jnp.concatenate([past_k.astype(F32), k_new], axis=1)
        v_all = jnp.concatenate([past_v.astype(F32), v_new], axis=1)
        k_pos = jnp.concatenate([jnp.arange(past_len, dtype=jnp.int32), pos])
    for j in range(N_B):
        l = N_A + j
        h, gate = modulate(x, c, norm_g[l], ada_w[l], ada_b[l])
        y = sb_layer(h, pos, k_all, v_all, k_pos, sb_w_in[j], sb_q_g[j], sb_bias[j], sb_w_out[j])
        x = x + gate * y
    return x, jnp.stack(new_states), k_new, v_new


def setup_inputs(seed: int = 0) -> dict:
    key = jax.random.key(seed)
    ks = jax.random.split(key, 24)
    n_pages = PAST_LEN // PAGE_SIZE
    n_used = DEC_BATCH * n_pages
    n_pool = n_used + max(n_used // 4, 1)
    nrm = jax.random.normal
    page_table = jax.random.permutation(ks[0], n_pool)[:n_used].reshape(DEC_BATCH, n_pages).astype(jnp.int32)
    return {
        "x_prompt": nrm(ks[1], (BATCH, SEQ, D_MODEL), F32),
        "x_sample": nrm(ks[2], (DEC_BATCH, DEC_SEQ, D_MODEL), F32),
        "state_ret": 0.1 * nrm(ks[3], (N_A, DEC_BATCH, RET_HEADS, RET_DK, RET_DV), F32),
        "cache_k": nrm(ks[4], (n_pool, PAGE_SIZE, SB_HEADS, SB_DH), F32),
        "cache_v": nrm(ks[5], (n_pool, PAGE_SIZE, SB_HEADS, SB_DH), F32),
        "page_table": page_table,
        "c_prompt": nrm(ks[6], (BATCH, D_MODEL), F32),
        "c_sample": nrm(ks[7], (DEC_BATCH, D_MODEL), F32),
        "ada_w": nrm(ks[8], (DEPTH, D_MODEL, 3 * D_MODEL), F32) * D_MODEL ** -0.5,
        "ada_b": 0.01 * nrm(ks[9], (DEPTH, 3 * D_MODEL), F32),
        "norm_g": 1.0 + 0.05 * nrm(ks[10], (DEPTH, D_MODEL), F32),
        "ret_w_in": nrm(ks[11], (N_A, D_MODEL, RET_IN_W), F32) * D_MODEL ** -0.5,
        "ret_norm_g": 1.0 + 0.05 * nrm(ks[12], (N_A, RET_V_W), F32),
        "ret_w_out": nrm(ks[13], (N_A, RET_V_W, D_MODEL), F32) * RET_V_W ** -0.5,
        "kv_norm_g": 1.0 + 0.05 * nrm(ks[14], (D_MODEL,), F32),
        "w_kv": nrm(ks[15], (D_MODEL, 2 * SB_W), F32) * D_MODEL ** -0.5,
        "sb_q_g": 1.0 + 0.05 * nrm(ks[16], (N_B, SB_DH), F32),
        "sb_k_g": 1.0 + 0.05 * nrm(ks[17], (SB_DH,), F32),
        "sb_w_in": nrm(ks[18], (N_B, D_MODEL, SB_IN_W), F32) * D_MODEL ** -0.5,
        "sb_w_out": nrm(ks[19], (N_B, SB_W, D_MODEL), F32) * SB_W ** -0.5,
        "sb_bias": SB_BIAS_INIT + 0.5 * nrm(ks[20], (N_B, SB_HEADS), F32),
    }


def reference(x_prompt, x_sample, state_ret, cache_k, cache_v, page_table, c_prompt, c_sample,
              ada_w, ada_b, norm_g, ret_w_in, ret_norm_g, ret_w_out,
              kv_norm_g, w_kv, sb_q_g, sb_k_g, sb_w_in, sb_w_out, sb_bias):
    weights = (ada_w, ada_b, norm_g, ret_w_in, ret_norm_g, ret_w_out,
               kv_norm_g, w_kv, sb_q_g, sb_k_g, sb_w_in, sb_w_out, sb_bias)
    L_p = x_prompt.shape[1]
    pos_p = jnp.arange(L_p, dtype=jnp.int32)
    y_p, st_p, k_p, v_p = trunk(x_prompt, c_prompt, pos_p, None, None, None, *weights)
    db, n_pages = page_table.shape
    past_len = n_pages * PAGE_SIZE
    past_k = cache_k[page_table].reshape(db, past_len, SB_HEADS, SB_DH)
    past_v = cache_v[page_table].reshape(db, past_len, SB_HEADS, SB_DH)
    L_s = x_sample.shape[1]
    pos_s = past_len + jnp.arange(L_s, dtype=jnp.int32)
    y_s, st_s, k_s, v_s = trunk(x_sample, c_sample, pos_s, state_ret, past_k, past_v, *weights)
    return (y_p.astype(x_prompt.dtype), y_s.astype(x_sample.dtype), st_p, st_s, k_p, v_p, k_s, v_s)
```

```python
import functools
import math

import jax
import jax.numpy as jnp
from jax import lax
from jax.experimental import pallas as pl
from jax.experimental.pallas import tpu as pltpu

F32 = jnp.float32
BF16 = jnp.bfloat16

RET_HEADS = 8
RET_DK = 128
RET_DV = 256
RET_CHUNK = 128
SB_HEADS = 16
SB_DH = 64
PAGE_SIZE = 128
ROPE_BASE = 10000.0
EPS = 1e-6
LOG2E = 1.4426950408889634

LANES = 128
SUBLANES = 8
VMEM_LIMIT_BYTES = 56 * 1024 * 1024

TOKEN_BLOCK = 512
SB_KEY_BLOCK = 256
SB_QUERY_BLOCK = 256
ADA_COL_BLOCK = 512
EXP2_CLAMP = 100.0


def _params(*sem):
    return pltpu.CompilerParams(dimension_semantics=sem, vmem_limit_bytes=VMEM_LIMIT_BYTES)


def _resident(a, n_grid):
    zeros = (0,) * a.ndim
    index_map = {1: lambda t: zeros, 2: lambda b, t: zeros}[n_grid]
    return pl.BlockSpec(a.shape, index_map, pipeline_mode=pl.Buffered(1))


def _split_bf16(a):
    hi = a.astype(BF16)
    lo = (a - hi.astype(F32)).astype(BF16)
    return hi, lo


def _dot(a, b):
    return jnp.dot(a, b, preferred_element_type=F32)


def _dot3(a, b):
    a_hi, a_lo = _split_bf16(a)
    b_hi, b_lo = _split_bf16(b)
    return _dot(a_hi, b_hi) + _dot(a_lo, b_hi) + _dot(a_hi, b_lo)


def _silu(x):
    return x / (1.0 + jnp.exp(-x))


def _ada_body(c_ref, w_ref, b_ref, o_ref):
    o_ref[...] = _dot3(_silu(c_ref[...]), w_ref[...]) + b_ref[...]


def _ada_call(c_pad, ada_w, ada_b):
    depth, d, d3 = ada_w.shape
    rows = c_pad.shape[0]
    tn = ADA_COL_BLOCK
    return pl.pallas_call(
        _ada_body,
        grid=(depth, d3 // tn),
        in_specs=[
            pl.BlockSpec((rows, d), lambda l, j: (0, 0)),
            pl.BlockSpec((None, d, tn), lambda l, j: (l, 0, j)),
            pl.BlockSpec((None, 1, tn), lambda l, j: (l, 0, j)),
        ],
        out_specs=pl.BlockSpec((None, rows, tn), lambda l, j: (l, 0, j)),
        out_shape=jax.ShapeDtypeStruct((depth, rows, d3), F32),
        compiler_params=_params("parallel", "parallel"),
        name="ada_mod",
    )(c_pad, ada_w, ada_b.reshape(depth, 1, d3))


def _modulated(x, g, m):
    d = x.shape[-1]
    r = lax.rsqrt(jnp.mean(x * x, axis=-1, keepdims=True) + EPS)
    return x * r * g * (1.0 + m[:, d:2 * d]) + m[:, :d]


def _mod_spec(mod, tm, blocks_per_seq):
    d3 = mod.shape[-1]
    if mod.shape[1] == 1:
        return pl.BlockSpec((None, 1, d3), lambda t: (t // blocks_per_seq, 0, 0))
    return pl.BlockSpec((None, tm, d3), lambda t: (0, t, 0))


def _tab_spec(tab, tm, blocks_per_seq):
    if blocks_per_seq is None:
        return pl.BlockSpec((tm, LANES), lambda t: (t, 0))
    return pl.BlockSpec((tm, LANES), lambda t: (t % blocks_per_seq, 0))


def _head_sum64(x_sq, ones_bd):
    hi, lo = _split_bf16(x_sq)
    return _dot(hi, ones_bd) + _dot(lo, ones_bd)


def _ret_proj_body(x_ref, m_ref, g_ref, w_ref, cos_ref, sin_ref, q_ref, k_ref, v_ref, sg_ref):
    h = _modulated(x_ref[...], g_ref[...], m_ref[...]).astype(BF16)
    cosf = cos_ref[...]
    sinf = sin_ref[...]
    qk_w = RET_HEADS * RET_DK
    v_w = RET_HEADS * RET_DV

    def rotary(p, scale):
        outs = []
        for hd in range(RET_HEADS):
            sl = p[:, hd * RET_DK:(hd + 1) * RET_DK]
            rot = pltpu.roll(sl, RET_DK // 2, axis=1)
            o = sl * cosf + rot * sinf
            outs.append(o if scale is None else o * scale)
        return jnp.concatenate(outs, axis=1)

    q = _dot(h, w_ref[:, 0:qk_w])
    q_ref[...] = rotary(q, None).astype(q_ref.dtype)
    k = _dot(h, w_ref[:, qk_w:2 * qk_w])
    k_ref[...] = rotary(k, RET_DK ** -0.5).astype(k_ref.dtype)
    v_ref[...] = _dot(h, w_ref[:, 2 * qk_w:2 * qk_w + v_w]).astype(v_ref.dtype)
    g = _dot(h, w_ref[:, 2 * qk_w + v_w:2 * qk_w + 2 * v_w])
    sg_ref[...] = _silu(g).astype(sg_ref.dtype)


def _ret_proj_call(x, mod, g, w_bf16, cosf, sinf, blocks_per_seq, out_dtype):
    t_total, d = x.shape
    tm = min(TOKEN_BLOCK, t_total)
    qk_w = RET_HEADS * RET_DK
    v_w = RET_HEADS * RET_DV
    row = lambda w: pl.BlockSpec((tm, w), lambda t: (t, 0))
    return pl.pallas_call(
        _ret_proj_body,
        grid=(t_total // tm,),
        in_specs=[
            row(d),
            _mod_spec(mod, tm, blocks_per_seq),
            pl.BlockSpec((1, d), lambda t: (0, 0)),
            _resident(w_bf16, 1),
            _tab_spec(cosf, tm, blocks_per_seq if mod.shape[1] == 1 else None),
            _tab_spec(sinf, tm, blocks_per_seq if mod.shape[1] == 1 else None),
        ],
        out_specs=[row(qk_w), row(qk_w), row(v_w), row(v_w)],
        out_shape=[
            jax.ShapeDtypeStruct((t_total, qk_w), out_dtype),
            jax.ShapeDtypeStruct((t_total, qk_w), out_dtype),
            jax.ShapeDtypeStruct((t_total, v_w), out_dtype),
            jax.ShapeDtypeStruct((t_total, v_w), out_dtype),
        ],
        compiler_params=_params("parallel"),
        name="ret_proj",
    )(x, mod, g.reshape(1, d), w_bf16, cosf, sinf)


def _ret_core_body(*refs, chunk, chunk_pad, n_chunks, has_s0):
    if has_s0:
        (q_ref, k_ref, v_ref, sg_ref, dm_ref, qd_ref, kd_ref, cd_ref, ng_ref, s0_ref,
         og_ref, s_ref) = refs
    else:
        (q_ref, k_ref, v_ref, sg_ref, dm_ref, qd_ref, kd_ref, cd_ref, ng_ref,
         og_ref, s_ref) = refs
        s0_ref = None
    mxu = BF16 if chunk >= 16 else F32

    @pl.when(pl.program_id(1) == 0)
    def _():
        if has_s0:
            s_ref[...] = s0_ref[...]
        else:
            s_ref[...] = jnp.zeros_like(s_ref)

    def pad_rows(a):
        if chunk_pad == chunk:
            return a
        return jnp.concatenate([a, jnp.zeros((chunk_pad - chunk, a.shape[1]), a.dtype)], axis=0)

    def one_chunk(ci, carry):
        r0 = pl.multiple_of(ci * chunk, chunk)
        rows = pl.ds(r0, chunk)
        for hd in range(RET_HEADS):
            qc = q_ref[rows, hd * RET_DK:(hd + 1) * RET_DK]
            kc = pad_rows(k_ref[rows, hd * RET_DK:(hd + 1) * RET_DK])
            vc = pad_rows(v_ref[rows, hd * RET_DV:(hd + 1) * RET_DV]).astype(mxu)
            s_old = s_ref[hd]
            scores = lax.dot_general(qc.astype(mxu), kc.astype(mxu), (((1,), (1,)), ((), ())),
                                     preferred_element_type=F32) * dm_ref[hd]
            intra = _dot(scores.astype(mxu), vc)
            qdec = (qc.astype(F32) * qd_ref[hd]).astype(mxu)
            cross = _dot(qdec, s_old.astype(mxu))
            o = intra + cross
            kdec = (kc.astype(F32) * kd_ref[hd]).astype(mxu)
            s_ref[hd] = s_old * cd_ref[hd] + lax.dot_general(
                kdec, vc, (((0,), (0,)), ((), ())), preferred_element_type=F32)
            r = lax.rsqrt(jnp.mean(o * o, axis=-1, keepdims=True) + EPS)
            cols = slice(hd * RET_DV, (hd + 1) * RET_DV)
            og = o * r * ng_ref[:, cols] * sg_ref[rows, cols].astype(F32)
            og_ref[rows, cols] = og.astype(og_ref.dtype)
        return carry

    if n_chunks == 1:
        one_chunk(0, 0)
    else:
        lax.fori_loop(0, n_chunks, one_chunk, 0)


def _ret_core_call(q, k, v, sg, tabs, norm_g, s0, layer, n_seq, out_dtype):
    t_total = q.shape[0]
    seq_len = t_total // n_seq
    chunk = math.gcd(seq_len, RET_CHUNK)
    chunk_pad = max(chunk, LANES)
    tm = min(TOKEN_BLOCK, seq_len)
    n_chunks = tm // chunk
    nblk = seq_len // tm
    dm, qd, kd, cd = tabs
    qk_w = RET_HEADS * RET_DK
    v_w = RET_HEADS * RET_DV
    row = lambda w: pl.BlockSpec((tm, w), lambda b, t: (b * nblk + t, 0))
    full = lambda a: _resident(a, 2)
    state_spec = pl.BlockSpec((None, RET_HEADS, RET_DK, RET_DV), lambda b, t: (b, 0, 0, 0))
    in_specs = [row(qk_w), row(qk_w), row(v_w), row(v_w), full(dm), full(qd), full(kd), full(cd),
                pl.BlockSpec((1, v_w), lambda b, t: (0, 0))]
    args = [q, k, v, sg, dm, qd, kd, cd, norm_g.reshape(1, v_w)]
    if s0 is not None:
        in_specs.append(pl.BlockSpec((None, None, RET_HEADS, RET_DK, RET_DV),
                                     lambda b, t: (layer, b, 0, 0, 0)))
        args.append(s0)
    body = functools.partial(_ret_core_body, chunk=chunk, chunk_pad=chunk_pad,
                             n_chunks=n_chunks, has_s0=s0 is not None)
    return pl.pallas_call(
        body,
        grid=(n_seq, nblk),
        in_specs=in_specs,
        out_specs=[row(v_w), state_spec],
        out_shape=[
            jax.ShapeDtypeStruct((t_total, v_w), out_dtype),
            jax.ShapeDtypeStruct((n_seq, RET_HEADS, RET_DK, RET_DV), F32),
        ],
        compiler_params=_params("parallel", "arbitrary"),
        name="ret_core",
    )(*args)


def _ret_tables(chunk, chunk_pad):
    hds = jnp.arange(RET_HEADS, dtype=F32)
    log_gamma = jnp.log1p(-jnp.exp2(-5.0 - hds))
    idx = jnp.arange(chunk, dtype=F32)
    diff = idx[:, None] - idx[None, :]
    dmask = jnp.where(diff[None] >= 0,
                      jnp.exp(log_gamma[:, None, None] * jnp.maximum(diff, 0.0)[None]), 0.0)
    dmask = jnp.pad(dmask, ((0, 0), (0, 0), (0, chunk_pad - chunk)))
    q_dec = jnp.exp(log_gamma[:, None] * (idx[None, :] + 1.0))
    k_dec = jnp.exp(log_gamma[:, None] * (chunk - 1.0 - idx[None, :]))
    k_dec = jnp.pad(k_dec, ((0, 0), (0, chunk_pad - chunk)))
    c_dec = jnp.exp(log_gamma * chunk)
    qd = jnp.broadcast_to(q_dec[:, :, None], (RET_HEADS, chunk, RET_DK))
    kd = jnp.broadcast_to(k_dec[:, :, None], (RET_HEADS, chunk_pad, RET_DK))
    cd = jnp.broadcast_to(c_dec[:, None, None], (RET_HEADS, 1, RET_DV))
    return dmask.astype(F32), qd.astype(F32), kd.astype(F32), cd.astype(F32)


def _out_proj_body(a_ref, w_ref, x_ref, m_ref, o_ref):
    d = x_ref.shape[-1]
    y = _dot(a_ref[...].astype(BF16), w_ref[...])
    o_ref[...] = x_ref[...] + m_ref[:, 2 * d:3 * d] * y


def _out_proj_call(a, w_bf16, x, mod, blocks_per_seq):
    t_total, d = x.shape
    tm = min(TOKEN_BLOCK, t_total)
    ka = a.shape[1]
    return pl.pallas_call(
        _out_proj_body,
        grid=(t_total // tm,),
        in_specs=[
            pl.BlockSpec((tm, ka), lambda t: (t, 0)),
            _resident(w_bf16, 1),
            pl.BlockSpec((tm, d), lambda t: (t, 0)),
            _mod_spec(mod, tm, blocks_per_seq),
        ],
        out_specs=pl.BlockSpec((tm, d), lambda t: (t, 0)),
        out_shape=jax.ShapeDtypeStruct((t_total, d), F32),
        compiler_params=_params("parallel"),
        name="out_proj",
    )(a, w_bf16, x, mod)


def _kv_body(*refs, permuted):
    if permuted:
        x_ref, g_ref, w_ref, kg_ref, ones_ref, perm_ref, k_ref, v_ref, kp_ref, vt_ref = refs
    else:
        x_ref, g_ref, w_ref, kg_ref, ones_ref, k_ref, v_ref = refs
    x = x_ref[...]
    sb_w = SB_HEADS * SB_DH
    r = lax.rsqrt(jnp.mean(x * x, axis=-1, keepdims=True) + EPS)
    xn = (x * r * g_ref[...]).astype(BF16)
    k = _dot(xn, w_ref[:, :sb_w])
    v = _dot(xn, w_ref[:, sb_w:])
    ms = _head_sum64(k * k, ones_ref[...]) * (1.0 / SB_DH)
    k = k * lax.rsqrt(ms + EPS) * kg_ref[...]
    k_ref[...] = k
    v_ref[...] = v
    if permuted:
        kb = SB_KEY_BLOCK
        perm = perm_ref[...]
        for j in range(x.shape[0] // kb):
            rows = slice(j * kb, (j + 1) * kb)
            kp_ref[rows, :] = _dot(perm, k[rows].astype(BF16)).astype(BF16)
            vt_ref[j] = _dot(perm, v[rows].astype(BF16)).T.astype(BF16)


def _kv_call(x, g, w_bf16, k_gain_row, ones_bd, perm, n_seq):
    t_total, d = x.shape
    tm = min(TOKEN_BLOCK, t_total)
    sb_w = SB_HEADS * SB_DH
    row = lambda w: pl.BlockSpec((tm, w), lambda t: (t, 0))
    const = lambda a: _resident(a, 1)
    in_specs = [row(d), pl.BlockSpec((1, d), lambda t: (0, 0)), const(w_bf16),
                const(k_gain_row), const(ones_bd)]
    args = [x, g.reshape(1, d), w_bf16, k_gain_row, ones_bd]
    out_specs = [row(sb_w), row(sb_w)]
    out_shape = [jax.ShapeDtypeStruct((t_total, sb_w), F32), jax.ShapeDtypeStruct((t_total, sb_w), F32)]
    if perm is not None:
        kb = SB_KEY_BLOCK
        assert tm % kb == 0
        in_specs.append(const(perm))
        args.append(perm)
        out_specs += [row(sb_w), pl.BlockSpec((tm // kb, sb_w, kb), lambda t: (t, 0, 0))]
        out_shape += [jax.ShapeDtypeStruct((t_total, sb_w), BF16),
                      jax.ShapeDtypeStruct((t_total // kb, sb_w, kb), BF16)]
    return pl.pallas_call(
        functools.partial(_kv_body, permuted=perm is not None),
        grid=(t_total // tm,),
        in_specs=in_specs,
        out_specs=out_specs,
        out_shape=out_shape,
        compiler_params=_params("parallel"),
        name="shared_kv",
    )(*args)


def _sb_proj_body(x_ref, m_ref, g_ref, w_ref, qg_ref, ones_ref, q_ref, sg_ref):
    h = _modulated(x_ref[...], g_ref[...], m_ref[...]).astype(BF16)
    sb_w = SB_HEADS * SB_DH
    q = _dot(h, w_ref[:, :sb_w])
    ms = _head_sum64(q * q, ones_ref[...]) * (1.0 / SB_DH)
    q_ref[...] = (q * lax.rsqrt(ms + EPS) * qg_ref[...]).astype(q_ref.dtype)
    sg_ref[...] = _silu(_dot(h, w_ref[:, sb_w:])).astype(sg_ref.dtype)


def _sb_proj_call(x, mod, g, w_bf16, q_gain_row, ones_bd, blocks_per_seq, out_dtype):
    t_total, d = x.shape
    tm = min(TOKEN_BLOCK, t_total)
    sb_w = SB_HEADS * SB_DH
    row = lambda w: pl.BlockSpec((tm, w), lambda t: (t, 0))
    const = lambda a: _resident(a, 1)
    return pl.pallas_call(
        _sb_proj_body,
        grid=(t_total // tm,),
        in_specs=[row(d), _mod_spec(mod, tm, blocks_per_seq), pl.BlockSpec((1, d), lambda t: (0, 0)),
                  const(w_bf16), const(q_gain_row), const(ones_bd)],
        out_specs=[row(sb_w), row(sb_w)],
        out_shape=[jax.ShapeDtypeStruct((t_total, sb_w), out_dtype),
                   jax.ShapeDtypeStruct((t_total, sb_w), out_dtype)],
        compiler_params=_params("parallel"),
        name="sb_proj",
    )(x, mod, g.reshape(1, d), w_bf16, q_gain_row, ones_bd)


def _sb_block(z_t, carry, mask):
    kb, nq = z_t.shape
    n = kb // SUBLANES
    e = jnp.exp2(jnp.minimum(z_t, EXP2_CLAMP))
    if mask is not None:
        e = jnp.where(mask, e, 0.0)
    a = 1.0 + e
    run = None
    partial = [None] * n
    for i in range(n - 1, -1, -1):
        a_i = a[i * SUBLANES:(i + 1) * SUBLANES, :]
        run = a_i if run is None else run * a_i
        partial[i] = run
    sub = lax.broadcasted_iota(jnp.int32, (SUBLANES, nq), 0)
    inc = run
    for step in (1, 2, 4):
        shifted = pltpu.roll(inc, SUBLANES - step, axis=0)
        inc = inc * jnp.where(sub < SUBLANES - step, shifted, 1.0)
    after = jnp.where(sub < SUBLANES - 1, pltpu.roll(inc, SUBLANES - 1, axis=0), 1.0)
    scale = carry * after
    new_carry = carry * jnp.broadcast_to(inc[0:1, :], (SUBLANES, nq))
    w = [e[i * SUBLANES:(i + 1) * SUBLANES, :] / (partial[i] * scale) for i in range(n)]
    return jnp.concatenate(w, axis=0), new_carry


def _sb_key_offsets(kb, nq):
    row = lax.broadcasted_iota(jnp.int32, (kb, nq), 0)
    return (row & (SUBLANES - 1)) * (kb // SUBLANES) + (row >> 3)


def _sb_prompt_body(bias_ref, q_ref, kp_ref, vt_ref, sg_ref, og_ref):
    hp = pl.program_id(1)
    qi = pl.program_id(2)
    qb = q_ref.shape[0]
    kb = SB_KEY_BLOCK
    ratio = qb // kb
    q = q_ref[...]
    lane = lax.broadcasted_iota(jnp.int32, q.shape, 1)
    key_off = _sb_key_offsets(kb, qb)
    q_idx = lax.broadcasted_iota(jnp.int32, (kb, qb), 1)
    n_full = qi * ratio
    accs = []
    for j in range(2):
        bias2 = bias_ref[2 * hp + j]
        qm = jnp.where((lane >= j * SB_DH) & (lane < (j + 1) * SB_DH), q, jnp.zeros_like(q))

        def block(kj, carry, acc, mask):
            r0 = pl.multiple_of(kj * kb, kb)
            z_t = lax.dot_general(kp_ref[pl.ds(r0, kb), :], qm, (((1,), (1,)), ((), ())),
                                  preferred_element_type=F32) + bias2
            w, carry = _sb_block(z_t, carry, mask)
            acc = acc + _dot(vt_ref[kj, j * SB_DH:(j + 1) * SB_DH, :], w.astype(BF16))
            return carry, acc

        carry = jnp.ones((SUBLANES, qb), F32)
        acc = jnp.zeros((SB_DH, qb), F32)
        for dgl in range(ratio):
            mask = key_off + (ratio - 1 - dgl) * kb < q_idx
            carry, acc = block(n_full + (ratio - 1 - dgl), carry, acc, mask)

        def body(it, st):
            return block(n_full - 1 - it, st[0], st[1], None)

        carry, acc = lax.fori_loop(0, n_full, body, (carry, acc))
        accs.append(acc)
    o = jnp.concatenate(accs, axis=0).T
    og_ref[...] = (o * sg_ref[...].astype(F32)).astype(og_ref.dtype)


def _sb_prompt_call(q, kp, vt, sg, bias2, n_seq):
    t_total, sb_w = q.shape
    seq_len = t_total // n_seq
    qb = min(SB_QUERY_BLOCK, seq_len)
    kb = SB_KEY_BLOCK
    n_q = seq_len // qb
    n_k = seq_len // kb
    n_hp = sb_w // LANES
    grid_spec = pltpu.PrefetchScalarGridSpec(
        num_scalar_prefetch=1,
        grid=(n_seq, n_hp, n_q),
        in_specs=[
            pl.BlockSpec((qb, LANES), lambda b, hp, qi, bias: (b * n_q + qi, hp)),
            pl.BlockSpec((seq_len, LANES), lambda b, hp, qi, bias: (b, hp)),
            pl.BlockSpec((n_k, LANES, kb), lambda b, hp, qi, bias: (b, hp, 0)),
            pl.BlockSpec((qb, LANES), lambda b, hp, qi, bias: (b * n_q + qi, hp)),
        ],
        out_specs=pl.BlockSpec((qb, LANES), lambda b, hp, qi, bias: (b * n_q + qi, hp)),
    )
    return pl.pallas_call(
        _sb_prompt_body,
        grid_spec=grid_spec,
        out_shape=jax.ShapeDtypeStruct((t_total, sb_w), BF16),
        compiler_params=_params("parallel", "parallel", "arbitrary"),
        name="sb_attn_prompt",
    )(bias2, q, kp, vt, sg)


def _sb_sample_body(pt_ref, q_ref, kn_ref, vn_ref, sg_ref, bias_ref, *rest, n_pages, n_new):
    k_pages = rest[:n_pages]
    v_pages = rest[n_pages:2 * n_pages]
    og_ref = rest[2 * n_pages]
    kb_ref, vb_ref, z_ref, w_ref = rest[2 * n_pages + 1:]
    del pt_ref
    past = n_pages * PAGE_SIZE
    sb_w = SB_HEADS * SB_DH
    n_lane = SB_HEADS * n_new
    pad_rows = PAGE_SIZE - n_new

    for p in range(n_pages):
        kb_ref[p * PAGE_SIZE:(p + 1) * PAGE_SIZE, :] = k_pages[p][...].astype(BF16)
        vb_ref[p * PAGE_SIZE:(p + 1) * PAGE_SIZE, :] = v_pages[p][...].astype(BF16)
    zeros_pad = jnp.zeros((pad_rows, sb_w), F32)
    kb_ref[past:past + PAGE_SIZE, :] = jnp.concatenate([kn_ref[...], zeros_pad], axis=0).astype(BF16)
    vb_ref[past:past + PAGE_SIZE, :] = jnp.concatenate([vn_ref[...], zeros_pad], axis=0).astype(BF16)

    q = q_ref[...]
    q_rows = jnp.concatenate([q] * SB_HEADS, axis=0)
    row_head = lax.broadcasted_iota(jnp.int32, (n_lane, sb_w), 0) // n_new
    lane_head = lax.broadcasted_iota(jnp.int32, (n_lane, sb_w), 1) // SB_DH
    head_mask = row_head == lane_head
    q_bd = jnp.where(head_mask, q_rows, 0.0).T.astype(BF16)

    z_ref[...] = _dot(kb_ref[...], q_bd) + bias_ref[...]
    w_ref[past + n_new:past + PAGE_SIZE, :] = jnp.zeros((pad_rows, n_lane), F32)

    key_idx = lax.broadcasted_iota(jnp.int32, (n_new, n_lane), 0)
    q_idx = lax.broadcasted_iota(jnp.int32, (n_new, n_lane), 1) % n_new
    carry = jnp.ones((SUBLANES, n_lane), F32)
    w_new, carry = _sb_block(z_ref[past:past + n_new, :], carry, key_idx < q_idx)
    w_ref[past:past + n_new, :] = w_new

    def body(it, carry):
        r0 = pl.multiple_of(past - SUBLANES * (it + 1), SUBLANES)
        w_blk, carry = _sb_block(z_ref[pl.ds(r0, SUBLANES), :], carry, None)
        w_ref[pl.ds(r0, SUBLANES), :] = w_blk
        return carry

    lax.fori_loop(0, past // SUBLANES, body, carry, unroll=8)

    out = _dot(w_ref[...].T.astype(BF16), vb_ref[...])
    out = jnp.where(head_mask, out, 0.0)
    o = out[0:n_new, :]
    for hd in range(1, SB_HEADS):
        o = o + out[hd * n_new:(hd + 1) * n_new, :]
    og_ref[...] = o * sg_ref[...]


def _sb_sample_call(q, k_new, v_new, sg, bias_lanes, cache_k, cache_v, page_table, n_new):
    t_total, sb_w = q.shape
    n_seq, n_pages = page_table.shape
    n_pool = cache_k.shape[0]
    assert n_new == SUBLANES and SB_HEADS * n_new == LANES
    ck = cache_k.reshape(n_pool, PAGE_SIZE, sb_w)
    cv = cache_v.reshape(n_pool, PAGE_SIZE, sb_w)
    n_lane = SB_HEADS * n_new
    rows_pad = (n_pages + 1) * PAGE_SIZE
    row = pl.BlockSpec((n_new, sb_w), lambda b, pt: (b, 0))

    def page_spec(p):
        return pl.BlockSpec((None, PAGE_SIZE, sb_w), lambda b, pt: (pt[b, p], 0, 0))

    grid_spec = pltpu.PrefetchScalarGridSpec(
        num_scalar_prefetch=1,
        grid=(n_seq,),
        in_specs=[row, row, row, row, pl.BlockSpec((1, n_lane), lambda b, pt: (0, 0))]
        + [page_spec(p) for p in range(n_pages)] + [page_spec(p) for p in range(n_pages)],
        out_specs=row,
        scratch_shapes=[
            pltpu.VMEM((rows_pad, sb_w), BF16),
            pltpu.VMEM((rows_pad, sb_w), BF16),
            pltpu.VMEM((rows_pad, n_lane), F32),
            pltpu.VMEM((rows_pad, n_lane), F32),
        ],
    )
    return pl.pallas_call(
        functools.partial(_sb_sample_body, n_pages=n_pages, n_new=n_new),
        grid_spec=grid_spec,
        out_shape=jax.ShapeDtypeStruct((t_total, sb_w), F32),
        compiler_params=_params("arbitrary"),
        name="sb_attn_sample",
    )(page_table, q, k_new, v_new, sg, bias_lanes, *([ck] * n_pages), *([cv] * n_pages))


def _rope_tables(pos):
    half = RET_DK // 2
    inv_freq = ROPE_BASE ** (-jnp.arange(half, dtype=F32) / half)
    ang = pos.astype(F32)[:, None] * inv_freq[None, :]
    cos, sin = jnp.cos(ang), jnp.sin(ang)
    return jnp.concatenate([cos, cos], axis=1), jnp.concatenate([-sin, sin], axis=1)


def _sb_perm_matrix(kb):
    row = jnp.arange(kb)
    src = (row % SUBLANES) * (kb // SUBLANES) + row // SUBLANES
    return (src[:, None] == jnp.arange(kb)[None, :]).astype(BF16)


def _trunk(x, mods, blocks_per_seq, n_seq, rope, s0, attn_fn, w, act_dtype, permuted):
    (norm_g, ret_w_in, ret_norm_g, ret_w_out, kv_norm_g, w_kv, sb_q_g, sb_k_g,
     sb_w_in, sb_w_out, ones_bd) = w
    n_a = ret_w_in.shape[0]
    n_b = sb_w_in.shape[0]
    seq_len = x.shape[0] // n_seq
    chunk = math.gcd(seq_len, RET_CHUNK)
    tabs = _ret_tables(chunk, max(chunk, LANES))
    states = []
    for l in range(n_a):
        q, k, v, sg = _ret_proj_call(x, mods[l], norm_g[l], ret_w_in[l], rope[0], rope[1],
                                     blocks_per_seq, act_dtype)
        og, st = _ret_core_call(q, k, v, sg, tabs, ret_norm_g[l], s0, l, n_seq, act_dtype)
        x = _out_proj_call(og, ret_w_out[l], x, mods[l], blocks_per_seq)
        states.append(st)
    k_gain_row = jnp.tile(sb_k_g.astype(F32), SB_HEADS).reshape(1, -1)
    perm = _sb_perm_matrix(SB_KEY_BLOCK) if permuted else None
    kv = _kv_call(x, kv_norm_g, w_kv, k_gain_row, ones_bd, perm, n_seq)
    for j in range(n_b):
        l = n_a + j
        q_gain_row = jnp.tile(sb_q_g[j].astype(F32) * (SB_DH ** -0.5 * LOG2E), SB_HEADS).reshape(1, -1)
        q, sg = _sb_proj_call(x, mods[l], norm_g[l], sb_w_in[j], q_gain_row, ones_bd,
                              blocks_per_seq, act_dtype)
        og = attn_fn(j, q, sg, kv)
        x = _out_proj_call(og, sb_w_out[j], x, mods[l], blocks_per_seq)
    return x, jnp.stack(states), kv[0], kv[1]


def kernel(x_prompt, x_sample, state_ret, cache_k, cache_v, page_table, c_prompt, c_sample,
           ada_w, ada_b, norm_g, ret_w_in, ret_norm_g, ret_w_out,
           kv_norm_g, w_kv, sb_q_g, sb_k_g, sb_w_in, sb_w_out, sb_bias):
    n_p, len_p, d = x_prompt.shape
    n_s, len_s, _ = x_sample.shape
    depth = ada_w.shape[0]
    sb_w = SB_HEADS * SB_DH
    past_len = page_table.shape[1] * PAGE_SIZE

    n_c = n_p + n_s
    c_rows = -(-n_c // 16) * 16
    c_all = jnp.concatenate([c_prompt.astype(F32), c_sample.astype(F32),
                             jnp.zeros((c_rows - n_c, d), F32)], axis=0)
    mod = _ada_call(c_all, ada_w.astype(F32), ada_b.astype(F32))
    mods_p = [mod[l, :n_p].reshape(n_p, 1, 3 * d) for l in range(depth)]
    mods_s = [jnp.repeat(mod[l, n_p:n_c], len_s, axis=0).reshape(1, n_s * len_s, 3 * d)
              for l in range(depth)]

    ones_bd = (jnp.arange(sb_w)[:, None] // SB_DH == jnp.arange(sb_w)[None, :] // SB_DH).astype(BF16)
    weights = (norm_g.astype(F32), ret_w_in.astype(BF16), ret_norm_g.astype(F32), ret_w_out.astype(BF16),
               kv_norm_g.astype(F32), w_kv.astype(BF16), sb_q_g, sb_k_g,
               sb_w_in.astype(BF16), sb_w_out.astype(BF16), ones_bd)
    bias2 = sb_bias.astype(F32) * LOG2E

    rope_p = _rope_tables(jnp.arange(len_p, dtype=jnp.int32))
    tm_p = min(TOKEN_BLOCK, len_p)

    def attn_p(j, q, sg, kv):
        return _sb_prompt_call(q, kv[2], kv[3], sg, bias2[j], n_p)

    y_p, st_p, k_p, v_p = _trunk(x_prompt.astype(F32).reshape(n_p * len_p, d), mods_p, len_p // tm_p,
                                 n_p, rope_p, None, attn_p, weights, BF16, True)

    pos_s = past_len + jnp.arange(len_s, dtype=jnp.int32)
    rope_s = tuple(jnp.tile(t, (n_s, 1)) for t in _rope_tables(pos_s))

    def attn_s(j, q, sg, kv):
        bias_lanes = jnp.repeat(bias2[j], len_s).reshape(1, SB_HEADS * len_s)
        return _sb_sample_call(q, kv[0], kv[1], sg, bias_lanes, cache_k.astype(F32),
                               cache_v.astype(F32), page_table, len_s)

    y_s, st_s, k_s, v_s = _trunk(x_sample.astype(F32).reshape(n_s * len_s, d), mods_s, None,
                                 n_s, rope_s, state_ret.astype(F32), attn_s, weights, F32, False)

    return (y_p.reshape(n_p, len_p, d).astype(x_prompt.dtype),
            y_s.reshape(n_s, len_s, d).astype(x_sample.dtype),
            st_p, st_s,
            k_p.reshape(n_p, len_p, SB_HEADS, SB_DH), v_p.reshape(n_p, len_p, SB_HEADS, SB_DH),
            k_s.reshape(n_s, len_s, SB_HEADS, SB_DH), v_s.reshape(n_s, len_s, SB_HEADS, SB_DH))
```

```python
import functools
import math

import jax
import jax.numpy as jnp
from jax import lax
from jax.experimental import pallas as pl
from jax.experimental.pallas import tpu as pltpu

F32 = jnp.float32
BF16 = jnp.bfloat16

RET_HEADS = 8
RET_DK = 128
RET_DV = 256
RET_CHUNK = 128
SB_HEADS = 16
SB_DH = 64
PAGE_SIZE = 128
ROPE_BASE = 10000.0
EPS = 1e-6
LOG2E = 1.4426950408889634

LANES = 128
SUBLANES = 8
VMEM_LIMIT_BYTES = 56 * 1024 * 1024

TOKEN_BLOCK = 512
SB_KEY_BLOCK = 256
SB_QUERY_BLOCK = 512
SB_LANE_GROUP = 256
ADA_COL_BLOCK = 512
EXP2_CLAMP = 100.0


def _params(*sem):
    return pltpu.CompilerParams(dimension_semantics=sem, vmem_limit_bytes=VMEM_LIMIT_BYTES)


def _resident(a, n_grid):
    zeros = (0,) * a.ndim
    index_map = {1: lambda t: zeros, 2: lambda b, t: zeros}[n_grid]
    return pl.BlockSpec(a.shape, index_map, pipeline_mode=pl.Buffered(1))


def _split_bf16(a):
    hi = a.astype(BF16)
    lo = (a - hi.astype(F32)).astype(BF16)
    return hi, lo


def _dot(a, b):
    return jnp.dot(a, b, preferred_element_type=F32)


def _dot3(a, b):
    a_hi, a_lo = _split_bf16(a)
    b_hi, b_lo = _split_bf16(b)
    return _dot(a_hi, b_hi) + _dot(a_lo, b_hi) + _dot(a_hi, b_lo)


def _silu(x):
    return x / (1.0 + jnp.exp(-x))


def _ada_body(c_ref, w_ref, b_ref, o_ref):
    o_ref[...] = _dot3(_silu(c_ref[...]), w_ref[...]) + b_ref[...]


def _ada_call(c_pad, ada_w, ada_b):
    depth, d, d3 = ada_w.shape
    rows = c_pad.shape[0]
    tn = ADA_COL_BLOCK
    return pl.pallas_call(
        _ada_body,
        grid=(depth, d3 // tn),
        in_specs=[
            pl.BlockSpec((rows, d), lambda l, j: (0, 0)),
            pl.BlockSpec((None, d, tn), lambda l, j: (l, 0, j)),
            pl.BlockSpec((None, 1, tn), lambda l, j: (l, 0, j)),
        ],
        out_specs=pl.BlockSpec((None, rows, tn), lambda l, j: (l, 0, j)),
        out_shape=jax.ShapeDtypeStruct((depth, rows, d3), F32),
        compiler_params=_params("parallel", "parallel"),
        name="ada_mod",
    )(c_pad, ada_w, ada_b.reshape(depth, 1, d3))


def _modulated(x, g, m):
    d = x.shape[-1]
    r = lax.rsqrt(jnp.mean(x * x, axis=-1, keepdims=True) + EPS)
    return x * r * g * (1.0 + m[:, d:2 * d]) + m[:, :d]


def _mod_spec(mod, tm, blocks_per_seq):
    d3 = mod.shape[-1]
    if mod.shape[1] == 1:
        return pl.BlockSpec((None, 1, d3), lambda t: (t // blocks_per_seq, 0, 0))
    return pl.BlockSpec((None, tm, d3), lambda t: (0, t, 0))


def _tab_spec(tab, tm, blocks_per_seq):
    if blocks_per_seq is None:
        return pl.BlockSpec((tm, LANES), lambda t: (t, 0))
    return pl.BlockSpec((tm, LANES), lambda t: (t % blocks_per_seq, 0))


def _head_sum64(x_sq, ones_bd):
    hi, lo = _split_bf16(x_sq)
    return _dot(hi, ones_bd) + _dot(lo, ones_bd)


def _ret_proj_body(x_ref, m_ref, g_ref, w_ref, cos_ref, sin_ref, q_ref, k_ref, v_ref, sg_ref):
    h = _modulated(x_ref[...], g_ref[...], m_ref[...]).astype(BF16)
    cosf = cos_ref[...]
    sinf = sin_ref[...]
    qk_w = RET_HEADS * RET_DK
    v_w = RET_HEADS * RET_DV

    def rotary(p, scale):
        outs = []
        for hd in range(RET_HEADS):
            sl = p[:, hd * RET_DK:(hd + 1) * RET_DK]
            rot = pltpu.roll(sl, RET_DK // 2, axis=1)
            o = sl * cosf + rot * sinf
            outs.append(o if scale is None else o * scale)
        return jnp.concatenate(outs, axis=1)

    q = _dot(h, w_ref[:, 0:qk_w])
    q_ref[...] = rotary(q, None).astype(q_ref.dtype)
    k = _dot(h, w_ref[:, qk_w:2 * qk_w])
    k_ref[...] = rotary(k, RET_DK ** -0.5).astype(k_ref.dtype)
    v_ref[...] = _dot(h, w_ref[:, 2 * qk_w:2 * qk_w + v_w]).astype(v_ref.dtype)
    g = _dot(h, w_ref[:, 2 * qk_w + v_w:2 * qk_w + 2 * v_w])
    sg_ref[...] = _silu(g).astype(sg_ref.dtype)


def _ret_proj_call(x, mod, g, w_bf16, cosf, sinf, blocks_per_seq, out_dtype):
    t_total, d = x.shape
    tm = min(TOKEN_BLOCK, t_total)
    qk_w = RET_HEADS * RET_DK
    v_w = RET_HEADS * RET_DV
    row = lambda w: pl.BlockSpec((tm, w), lambda t: (t, 0))
    return pl.pallas_call(
        _ret_proj_body,
        grid=(t_total // tm,),
        in_specs=[
            row(d),
            _mod_spec(mod, tm, blocks_per_seq),
            pl.BlockSpec((1, d), lambda t: (0, 0)),
            _resident(w_bf16, 1),
            _tab_spec(cosf, tm, blocks_per_seq if mod.shape[1] == 1 else None),
            _tab_spec(sinf, tm, blocks_per_seq if mod.shape[1] == 1 else None),
        ],
        out_specs=[row(qk_w), row(qk_w), row(v_w), row(v_w)],
        out_shape=[
            jax.ShapeDtypeStruct((t_total, qk_w), out_dtype),
            jax.ShapeDtypeStruct((t_total, qk_w), out_dtype),
            jax.ShapeDtypeStruct((t_total, v_w), out_dtype),
            jax.ShapeDtypeStruct((t_total, v_w), out_dtype),
        ],
        compiler_params=_params("parallel"),
        name="ret_proj",
    )(x, mod, g.reshape(1, d), w_bf16, cosf, sinf)


def _ret_core_body(*refs, chunk, chunk_pad, n_chunks, has_s0):
    if has_s0:
        (q_ref, k_ref, v_ref, sg_ref, dm_ref, qd_ref, kd_ref, cd_ref, ng_ref, s0_ref,
         og_ref, s_ref) = refs
    else:
        (q_ref, k_ref, v_ref, sg_ref, dm_ref, qd_ref, kd_ref, cd_ref, ng_ref,
         og_ref, s_ref) = refs
        s0_ref = None
    mxu = BF16 if chunk >= 16 else F32

    @pl.when(pl.program_id(1) == 0)
    def _():
        if has_s0:
            s_ref[...] = s0_ref[...]
        else:
            s_ref[...] = jnp.zeros_like(s_ref)

    def pad_rows(a):
        if chunk_pad == chunk:
            return a
        return jnp.concatenate([a, jnp.zeros((chunk_pad - chunk, a.shape[1]), a.dtype)], axis=0)

    def one_chunk(ci, carry):
        r0 = pl.multiple_of(ci * chunk, chunk)
        rows = pl.ds(r0, chunk)
        for hd in range(RET_HEADS):
            qc = q_ref[rows, hd * RET_DK:(hd + 1) * RET_DK]
            kc = pad_rows(k_ref[rows, hd * RET_DK:(hd + 1) * RET_DK])
            vc = pad_rows(v_ref[rows, hd * RET_DV:(hd + 1) * RET_DV]).astype(mxu)
            s_old = s_ref[hd]
            scores = lax.dot_general(qc.astype(mxu), kc.astype(mxu), (((1,), (1,)), ((), ())),
                                     preferred_element_type=F32) * dm_ref[hd]
            intra = _dot(scores.astype(mxu), vc)
            qdec = (qc.astype(F32) * qd_ref[hd]).astype(mxu)
            cross = _dot(qdec, s_old.astype(mxu))
            o = intra + cross
            kdec = (kc.astype(F32) * kd_ref[hd]).astype(mxu)
            s_ref[hd] = s_old * cd_ref[hd] + lax.dot_general(
                kdec, vc, (((0,), (0,)), ((), ())), preferred_element_type=F32)
            r = lax.rsqrt(jnp.mean(o * o, axis=-1, keepdims=True) + EPS)
            cols = slice(hd * RET_DV, (hd + 1) * RET_DV)
            og = o * r * ng_ref[:, cols] * sg_ref[rows, cols].astype(F32)
            og_ref[rows, cols] = og.astype(og_ref.dtype)
        return carry

    if n_chunks == 1:
        one_chunk(0, 0)
    else:
        lax.fori_loop(0, n_chunks, one_chunk, 0)


def _ret_core_call(q, k, v, sg, tabs, norm_g, s0, layer, n_seq, out_dtype):
    t_total = q.shape[0]
    seq_len = t_total // n_seq
    chunk = math.gcd(seq_len, RET_CHUNK)
    chunk_pad = max(chunk, LANES)
    tm = min(TOKEN_BLOCK, seq_len)
    n_chunks = tm // chunk
    nblk = seq_len // tm
    dm, qd, kd, cd = tabs
    qk_w = RET_HEADS * RET_DK
    v_w = RET_HEADS * RET_DV
    row = lambda w: pl.BlockSpec((tm, w), lambda b, t: (b * nblk + t, 0))
    full = lambda a: _resident(a, 2)
    state_spec = pl.BlockSpec((None, RET_HEADS, RET_DK, RET_DV), lambda b, t: (b, 0, 0, 0))
    in_specs = [row(qk_w), row(qk_w), row(v_w), row(v_w), full(dm), full(qd), full(kd), full(cd),
                pl.BlockSpec((1, v_w), lambda b, t: (0, 0))]
    args = [q, k, v, sg, dm, qd, kd, cd, norm_g.reshape(1, v_w)]
    if s0 is not None:
        in_specs.append(pl.BlockSpec((None, None, RET_HEADS, RET_DK, RET_DV),
                                     lambda b, t: (layer, b, 0, 0, 0)))
        args.append(s0)
    body = functools.partial(_ret_core_body, chunk=chunk, chunk_pad=chunk_pad,
                             n_chunks=n_chunks, has_s0=s0 is not None)
    return pl.pallas_call(
        body,
        grid=(n_seq, nblk),
        in_specs=in_specs,
        out_specs=[row(v_w), state_spec],
        out_shape=[
            jax.ShapeDtypeStruct((t_total, v_w), out_dtype),
            jax.ShapeDtypeStruct((n_seq, RET_HEADS, RET_DK, RET_DV), F32),
        ],
        compiler_params=_params("parallel", "arbitrary"),
        name="ret_core",
    )(*args)


def _ret_tables(chunk, chunk_pad):
    hds = jnp.arange(RET_HEADS, dtype=F32)
    log_gamma = jnp.log1p(-jnp.exp2(-5.0 - hds))
    idx = jnp.arange(chunk, dtype=F32)
    diff = idx[:, None] - idx[None, :]
    dmask = jnp.where(diff[None] >= 0,
                      jnp.exp(log_gamma[:, None, None] * jnp.maximum(diff, 0.0)[None]), 0.0)
    dmask = jnp.pad(dmask, ((0, 0), (0, 0), (0, chunk_pad - chunk)))
    q_dec = jnp.exp(log_gamma[:, None] * (idx[None, :] + 1.0))
    k_dec = jnp.exp(log_gamma[:, None] * (chunk - 1.0 - idx[None, :]))
    k_dec = jnp.pad(k_dec, ((0, 0), (0, chunk_pad - chunk)))
    c_dec = jnp.exp(log_gamma * chunk)
    qd = jnp.broadcast_to(q_dec[:, :, None], (RET_HEADS, chunk, RET_DK))
    kd = jnp.broadcast_to(k_dec[:, :, None], (RET_HEADS, chunk_pad, RET_DK))
    cd = jnp.broadcast_to(c_dec[:, None, None], (RET_HEADS, 1, RET_DV))
    return dmask.astype(F32), qd.astype(F32), kd.astype(F32), cd.astype(F32)


def _out_proj_body(a_ref, w_ref, x_ref, m_ref, o_ref):
    d = x_ref.shape[-1]
    y = _dot(a_ref[...].astype(BF16), w_ref[...])
    o_ref[...] = x_ref[...] + m_ref[:, 2 * d:3 * d] * y


def _out_proj_call(a, w_bf16, x, mod, blocks_per_seq):
    t_total, d = x.shape
    tm = min(TOKEN_BLOCK, t_total)
    ka = a.shape[1]
    return pl.pallas_call(
        _out_proj_body,
        grid=(t_total // tm,),
        in_specs=[
            pl.BlockSpec((tm, ka), lambda t: (t, 0)),
            _resident(w_bf16, 1),
            pl.BlockSpec((tm, d), lambda t: (t, 0)),
            _mod_spec(mod, tm, blocks_per_seq),
        ],
        out_specs=pl.BlockSpec((tm, d), lambda t: (t, 0)),
        out_shape=jax.ShapeDtypeStruct((t_total, d), F32),
        compiler_params=_params("parallel"),
        name="out_proj",
    )(a, w_bf16, x, mod)


def _kv_body(*refs, permuted):
    if permuted:
        x_ref, g_ref, w_ref, kg_ref, ones_ref, perm_ref, k_ref, v_ref, kp_ref, vt_ref = refs
    else:
        x_ref, g_ref, w_ref, kg_ref, ones_ref, k_ref, v_ref = refs
    x = x_ref[...]
    sb_w = SB_HEADS * SB_DH
    r = lax.rsqrt(jnp.mean(x * x, axis=-1, keepdims=True) + EPS)
    xn = (x * r * g_ref[...]).astype(BF16)
    k = _dot(xn, w_ref[:, :sb_w])
    v = _dot(xn, w_ref[:, sb_w:])
    ms = _head_sum64(k * k, ones_ref[...]) * (1.0 / SB_DH)
    k = k * lax.rsqrt(ms + EPS) * kg_ref[...]
    k_ref[...] = k
    v_ref[...] = v
    if permuted:
        kb = SB_KEY_BLOCK
        sup = SB_QUERY_BLOCK
        perm = perm_ref[...]
        for s in range(x.shape[0] // sup):
            cols = []
            for j in range(sup // kb):
                rows = slice(s * sup + j * kb, s * sup + (j + 1) * kb)
                kp_ref[rows, :] = _dot(perm, k[rows].astype(BF16)).astype(BF16)
                cols.append(_dot(perm, v[rows].astype(BF16)).T.astype(BF16))
            vt_ref[s] = jnp.concatenate(cols, axis=1)


def _kv_call(x, g, w_bf16, k_gain_row, ones_bd, perm, n_seq):
    t_total, d = x.shape
    tm = min(TOKEN_BLOCK, t_total)
    sb_w = SB_HEADS * SB_DH
    row = lambda w: pl.BlockSpec((tm, w), lambda t: (t, 0))
    const = lambda a: _resident(a, 1)
    in_specs = [row(d), pl.BlockSpec((1, d), lambda t: (0, 0)), const(w_bf16),
                const(k_gain_row), const(ones_bd)]
    args = [x, g.reshape(1, d), w_bf16, k_gain_row, ones_bd]
    out_specs = [row(sb_w), row(sb_w)]
    out_shape = [jax.ShapeDtypeStruct((t_total, sb_w), F32), jax.ShapeDtypeStruct((t_total, sb_w), F32)]
    if perm is not None:
        sup = SB_QUERY_BLOCK
        assert tm % sup == 0 and sup % SB_KEY_BLOCK == 0
        in_specs.append(const(perm))
        args.append(perm)
        out_specs += [row(sb_w), pl.BlockSpec((tm // sup, sb_w, sup), lambda t: (t, 0, 0))]
        out_shape += [jax.ShapeDtypeStruct((t_total, sb_w), BF16),
                      jax.ShapeDtypeStruct((t_total // sup, sb_w, sup), BF16)]
    return pl.pallas_call(
        functools.partial(_kv_body, permuted=perm is not None),
        grid=(t_total // tm,),
        in_specs=in_specs,
        out_specs=out_specs,
        out_shape=out_shape,
        compiler_params=_params("parallel"),
        name="shared_kv",
    )(*args)


def _sb_proj_body(x_ref, m_ref, g_ref, w_ref, qg_ref, ones_ref, q_ref, sg_ref):
    h = _modulated(x_ref[...], g_ref[...], m_ref[...]).astype(BF16)
    sb_w = SB_HEADS * SB_DH
    q = _dot(h, w_ref[:, :sb_w])
    ms = _head_sum64(q * q, ones_ref[...]) * (1.0 / SB_DH)
    q_ref[...] = (q * lax.rsqrt(ms + EPS) * qg_ref[...]).astype(q_ref.dtype)
    sg_ref[...] = _silu(_dot(h, w_ref[:, sb_w:])).astype(sg_ref.dtype)


def _sb_proj_call(x, mod, g, w_bf16, q_gain_row, ones_bd, blocks_per_seq, out_dtype):
    t_total, d = x.shape
    tm = min(TOKEN_BLOCK, t_total)
    sb_w = SB_HEADS * SB_DH
    row = lambda w: pl.BlockSpec((tm, w), lambda t: (t, 0))
    const = lambda a: _resident(a, 1)
    return pl.pallas_call(
        _sb_proj_body,
        grid=(t_total // tm,),
        in_specs=[row(d), _mod_spec(mod, tm, blocks_per_seq), pl.BlockSpec((1, d), lambda t: (0, 0)),
                  const(w_bf16), const(q_gain_row), const(ones_bd)],
        out_specs=[row(sb_w), row(sb_w)],
        out_shape=[jax.ShapeDtypeStruct((t_total, sb_w), out_dtype),
                   jax.ShapeDtypeStruct((t_total, sb_w), out_dtype)],
        compiler_params=_params("parallel"),
        name="sb_proj",
    )(x, mod, g.reshape(1, d), w_bf16, q_gain_row, ones_bd)


def _sb_blocks(zs, carry, masks):
    kb, nq = zs[0].shape
    n = kb // SUBLANES
    es = []
    for z_t, mask in zip(zs, masks):
        e = jnp.exp2(jnp.minimum(z_t, EXP2_CLAMP))
        es.append(e if mask is None else jnp.where(mask, e, 0.0))
    runs = [None] * len(zs)
    partial = [[None] * n for _ in zs]
    for i in range(n):
        for b, e in enumerate(es):
            a_i = 1.0 + e[i * SUBLANES:(i + 1) * SUBLANES, :]
            runs[b] = a_i if runs[b] is None else runs[b] * a_i
            partial[b][i] = runs[b]
    sub = lax.broadcasted_iota(jnp.int32, (SUBLANES, nq), 0)
    ws = []
    for b, e in enumerate(es):
        inc = runs[b]
        for step in (1, 2, 4):
            shifted = pltpu.roll(inc, SUBLANES - step, axis=0)
            inc = inc * jnp.where(sub < SUBLANES - step, shifted, 1.0)
        after = jnp.where(sub < SUBLANES - 1, pltpu.roll(inc, SUBLANES - 1, axis=0), 1.0)
        scale = carry * after
        carry = carry * jnp.broadcast_to(inc[0:1, :], (SUBLANES, nq))
        ws.append(jnp.concatenate(
            [e[i * SUBLANES:(i + 1) * SUBLANES, :] / (partial[b][i] * scale) for i in range(n)], axis=0))
    return ws, carry


def _sb_block(z_t, carry, mask):
    ws, carry = _sb_blocks([z_t], carry, [mask])
    return ws[0], carry


def _sb_key_offsets(kb, nq):
    row = lax.broadcasted_iota(jnp.int32, (kb, nq), 0)
    n = kb // SUBLANES
    return (row & (SUBLANES - 1)) * n + (n - 1 - (row >> 3))


def _sb_prompt_body(bias_ref, q_ref, kp_ref, vt_ref, sg_ref, og_ref, acc_ref):
    hp = pl.program_id(1)
    qi = pl.program_id(2)
    qb = q_ref.shape[0]
    kb = SB_KEY_BLOCK
    ratio = qb // kb
    q = q_ref[...]
    lane = lax.broadcasted_iota(jnp.int32, q.shape, 1)
    qms = [jnp.where((lane >= j * SB_DH) & (lane < (j + 1) * SB_DH), q, jnp.zeros_like(q))
           for j in range(2)]
    lg = SB_LANE_GROUP
    units = [(j, h) for j in range(2) for h in range(qb // lg)]
    q_unit = [qms[j][h * lg:(h + 1) * lg, :] for j, h in units]
    biases = [bias_ref[2 * hp + j] for j in range(2)]
    order = list(range(ratio - 1, -1, -1))

    def logits(ks, u):
        r0 = pl.multiple_of(ks * qb, qb)
        return lax.dot_general(kp_ref[pl.ds(r0, qb), :], q_unit[u], (((1,), (1,)), ((), ())),
                               preferred_element_type=F32) + biases[units[u][0]]

    def weights(z_t, carry, masks):
        ws, carry = _sb_blocks([z_t[c * kb:(c + 1) * kb, :] for c in order], carry,
                               [None if masks is None else masks[c] for c in order])
        by_block = dict(zip(order, ws))
        return jnp.concatenate([by_block[c] for c in range(ratio)], axis=0).astype(BF16), carry

    def accumulate(ks, u, w):
        j, h = units[u]
        acc_ref[j, :, h * lg:(h + 1) * lg] += _dot(vt_ref[ks, j * SB_DH:(j + 1) * SB_DH, :], w)

    def super_block(ks, ks_next, z_first, w_last, carries, masks):
        z_t, w_prev = z_first, w_last
        out = []
        for u in range(len(units)):
            z_ahead = logits(ks, u + 1) if u + 1 < len(units) else logits(ks_next, 0)
            if u > 0:
                accumulate(ks, u - 1, w_prev)
            elif w_prev is not None:
                accumulate(ks + 1, len(units) - 1, w_prev)
            w_prev, carry = weights(z_t, carries[u], None if masks is None else masks[units[u][1]])
            out.append(carry)
            z_t = z_ahead
        return z_t, w_prev, tuple(out)

    acc_ref[...] = jnp.zeros_like(acc_ref)
    key_off = _sb_key_offsets(kb, lg)
    q_idx = lax.broadcasted_iota(jnp.int32, (kb, lg), 1)
    masks = [[key_off + c * kb < q_idx + h * lg for c in range(ratio)] for h in range(qb // lg)]
    ones = jnp.ones((SUBLANES, lg), F32)
    state = super_block(qi, jnp.maximum(qi - 1, 0), logits(qi, 0), None, (ones,) * len(units), masks)

    def trip(it, state):
        ks = qi - 1 - it
        return super_block(ks, jnp.maximum(ks - 1, 0), state[0], state[1], state[2], None)

    _, w_last, _ = lax.fori_loop(0, qi, trip, state)
    accumulate(0, len(units) - 1, w_last)
    o = jnp.concatenate([acc_ref[0], acc_ref[1]], axis=0).T
    og_ref[...] = (o * sg_ref[...].astype(F32)).astype(og_ref.dtype)


def _sb_prompt_call(q, kp, vt, sg, bias2, n_seq):
    t_total, sb_w = q.shape
    seq_len = t_total // n_seq
    qb = SB_QUERY_BLOCK
    assert seq_len % qb == 0
    n_q = seq_len // qb
    n_hp = sb_w // LANES
    grid_spec = pltpu.PrefetchScalarGridSpec(
        num_scalar_prefetch=1,
        grid=(n_seq, n_hp, n_q),
        in_specs=[
            pl.BlockSpec((qb, LANES), lambda b, hp, qi, bias: (b * n_q + qi, hp)),
            pl.BlockSpec((seq_len, LANES), lambda b, hp, qi, bias: (b, hp)),
            pl.BlockSpec((n_q, LANES, qb), lambda b, hp, qi, bias: (b, hp, 0)),
            pl.BlockSpec((qb, LANES), lambda b, hp, qi, bias: (b * n_q + qi, hp)),
        ],
        out_specs=pl.BlockSpec((qb, LANES), lambda b, hp, qi, bias: (b * n_q + qi, hp)),
        scratch_shapes=[pltpu.VMEM((2, SB_DH, qb), F32)],
    )
    return pl.pallas_call(
        _sb_prompt_body,
        grid_spec=grid_spec,
        out_shape=jax.ShapeDtypeStruct((t_total, sb_w), BF16),
        compiler_params=_params("parallel", "parallel", "arbitrary"),
        name="sb_attn_prompt",
    )(bias2, q, kp, vt, sg)


def _sb_sample_body(pt_ref, q_ref, kn_ref, vn_ref, sg_ref, bias_ref, *rest, n_pages, n_new):
    k_pages = rest[:n_pages]
    v_pages = rest[n_pages:2 * n_pages]
    og_ref = rest[2 * n_pages]
    kb_ref, vb_ref, z_ref, w_ref = rest[2 * n_pages + 1:]
    del pt_ref
    past = n_pages * PAGE_SIZE
    sb_w = SB_HEADS * SB_DH
    n_lane = SB_HEADS * n_new
    pad_rows = PAGE_SIZE - n_new

    for p in range(n_pages):
        kb_ref[p * PAGE_SIZE:(p + 1) * PAGE_SIZE, :] = k_pages[p][...].astype(BF16)
        vb_ref[p * PAGE_SIZE:(p + 1) * PAGE_SIZE, :] = v_pages[p][...].astype(BF16)
    zeros_pad = jnp.zeros((pad_rows, sb_w), F32)
    kb_ref[past:past + PAGE_SIZE, :] = jnp.concatenate([kn_ref[...], zeros_pad], axis=0).astype(BF16)
    vb_ref[past:past + PAGE_SIZE, :] = jnp.concatenate([vn_ref[...], zeros_pad], axis=0).astype(BF16)

    q = q_ref[...]
    q_rows = jnp.concatenate([q] * SB_HEADS, axis=0)
    row_head = lax.broadcasted_iota(jnp.int32, (n_lane, sb_w), 0) // n_new
    lane_head = lax.broadcasted_iota(jnp.int32, (n_lane, sb_w), 1) // SB_DH
    head_mask = row_head == lane_head
    q_bd = jnp.where(head_mask, q_rows, 0.0).T.astype(BF16)

    z_ref[...] = _dot(kb_ref[...], q_bd) + bias_ref[...]
    w_ref[past + n_new:past + PAGE_SIZE, :] = jnp.zeros((pad_rows, n_lane), F32)

    key_idx = lax.broadcasted_iota(jnp.int32, (n_new, n_lane), 0)
    q_idx = lax.broadcasted_iota(jnp.int32, (n_new, n_lane), 1) % n_new
    carry = jnp.ones((SUBLANES, n_lane), F32)
    w_new, carry = _sb_block(z_ref[past:past + n_new, :], carry, key_idx < q_idx)
    w_ref[past:past + n_new, :] = w_new

    def body(it, carry):
        r0 = pl.multiple_of(past - SUBLANES * (it + 1), SUBLANES)
        w_blk, carry = _sb_block(z_ref[pl.ds(r0, SUBLANES), :], carry, None)
        w_ref[pl.ds(r0, SUBLANES), :] = w_blk
        return carry

    lax.fori_loop(0, past // SUBLANES, body, carry, unroll=8)

    out = _dot(w_ref[...].T.astype(BF16), vb_ref[...])
    out = jnp.where(head_mask, out, 0.0)
    o = out[0:n_new, :]
    for hd in range(1, SB_HEADS):
        o = o + out[hd * n_new:(hd + 1) * n_new, :]
    og_ref[...] = o * sg_ref[...]


def _sb_sample_call(q, k_new, v_new, sg, bias_lanes, ck, cv, page_table, n_new):
    t_total, sb_w = q.shape
    n_seq, n_pages = page_table.shape
    assert n_new == SUBLANES and SB_HEADS * n_new == LANES
    n_lane = SB_HEADS * n_new
    rows_pad = (n_pages + 1) * PAGE_SIZE
    row = pl.BlockSpec((n_new, sb_w), lambda b, pt: (b, 0))

    def page_spec(p):
        return pl.BlockSpec((None, PAGE_SIZE, sb_w), lambda b, pt: (pt[b, p], 0, 0))

    grid_spec = pltpu.PrefetchScalarGridSpec(
        num_scalar_prefetch=1,
        grid=(n_seq,),
        in_specs=[row, row, row, row, pl.BlockSpec((1, n_lane), lambda b, pt: (0, 0))]
        + [page_spec(p) for p in range(n_pages)] + [page_spec(p) for p in range(n_pages)],
        out_specs=row,
        scratch_shapes=[
            pltpu.VMEM((rows_pad, sb_w), BF16),
            pltpu.VMEM((rows_pad, sb_w), BF16),
            pltpu.VMEM((rows_pad, n_lane), F32),
            pltpu.VMEM((rows_pad, n_lane), F32),
        ],
    )
    return pl.pallas_call(
        functools.partial(_sb_sample_body, n_pages=n_pages, n_new=n_new),
        grid_spec=grid_spec,
        out_shape=jax.ShapeDtypeStruct((t_total, sb_w), F32),
        compiler_params=_params("arbitrary"),
        name="sb_attn_sample",
    )(page_table, q, k_new, v_new, sg, bias_lanes, *([ck] * n_pages), *([cv] * n_pages))


def _rope_tables(pos):
    half = RET_DK // 2
    inv_freq = ROPE_BASE ** (-jnp.arange(half, dtype=F32) / half)
    ang = pos.astype(F32)[:, None] * inv_freq[None, :]
    cos, sin = jnp.cos(ang), jnp.sin(ang)
    return jnp.concatenate([cos, cos], axis=1), jnp.concatenate([-sin, sin], axis=1)


def _sb_perm_matrix(kb):
    row = jnp.arange(kb)
    n = kb // SUBLANES
    src = (row % SUBLANES) * n + (n - 1 - row // SUBLANES)
    return (src[:, None] == jnp.arange(kb)[None, :]).astype(BF16)


def _trunk(x, mods, blocks_per_seq, n_seq, rope, s0, attn_fn, w, act_dtype, permuted):
    (norm_g, ret_w_in, ret_norm_g, ret_w_out, kv_norm_g, w_kv, sb_q_g, sb_k_g,
     sb_w_in, sb_w_out, ones_bd) = w
    n_a = ret_w_in.shape[0]
    n_b = sb_w_in.shape[0]
    seq_len = x.shape[0] // n_seq
    chunk = math.gcd(seq_len, RET_CHUNK)
    tabs = _ret_tables(chunk, max(chunk, LANES))
    states = []
    for l in range(n_a):
        q, k, v, sg = _ret_proj_call(x, mods[l], norm_g[l], ret_w_in[l], rope[0], rope[1],
                                     blocks_per_seq, act_dtype)
        og, st = _ret_core_call(q, k, v, sg, tabs, ret_norm_g[l], s0, l, n_seq, act_dtype)
        x = _out_proj_call(og, ret_w_out[l], x, mods[l], blocks_per_seq)
        states.append(st)
    k_gain_row = jnp.tile(sb_k_g.astype(F32), SB_HEADS).reshape(1, -1)
    perm = _sb_perm_matrix(SB_KEY_BLOCK) if permuted else None
    kv = _kv_call(x, kv_norm_g, w_kv, k_gain_row, ones_bd, perm, n_seq)
    for j in range(n_b):
        l = n_a + j
        q_gain_row = jnp.tile(sb_q_g[j].astype(F32) * (SB_DH ** -0.5 * LOG2E), SB_HEADS).reshape(1, -1)
        q, sg = _sb_proj_call(x, mods[l], norm_g[l], sb_w_in[j], q_gain_row, ones_bd,
                              blocks_per_seq, act_dtype)
        og = attn_fn(j, q, sg, kv)
        x = _out_proj_call(og, sb_w_out[j], x, mods[l], blocks_per_seq)
    return x, jnp.stack(states), kv[0], kv[1]


def kernel(x_prompt, x_sample, state_ret, cache_k, cache_v, page_table, c_prompt, c_sample,
           ada_w, ada_b, norm_g, ret_w_in, ret_norm_g, ret_w_out,
           kv_norm_g, w_kv, sb_q_g, sb_k_g, sb_w_in, sb_w_out, sb_bias):
    n_p, len_p, d = x_prompt.shape
    n_s, len_s, _ = x_sample.shape
    depth = ada_w.shape[0]
    sb_w = SB_HEADS * SB_DH
    past_len = page_table.shape[1] * PAGE_SIZE

    n_c = n_p + n_s
    c_rows = -(-n_c // 16) * 16
    c_all = jnp.concatenate([c_prompt.astype(F32), c_sample.astype(F32),
                             jnp.zeros((c_rows - n_c, d), F32)], axis=0)
    mod = _ada_call(c_all, ada_w.astype(F32), ada_b.astype(F32))
    mods_p = [mod[l, :n_p].reshape(n_p, 1, 3 * d) for l in range(depth)]
    mods_s = [jnp.repeat(mod[l, n_p:n_c], len_s, axis=0).reshape(1, n_s * len_s, 3 * d)
              for l in range(depth)]

    ones_bd = (jnp.arange(sb_w)[:, None] // SB_DH == jnp.arange(sb_w)[None, :] // SB_DH).astype(BF16)
    weights = (norm_g.astype(F32), ret_w_in.astype(BF16), ret_norm_g.astype(F32), ret_w_out.astype(BF16),
               kv_norm_g.astype(F32), w_kv.astype(BF16), sb_q_g, sb_k_g,
               sb_w_in.astype(BF16), sb_w_out.astype(BF16), ones_bd)
    bias2 = sb_bias.astype(F32) * LOG2E

    rope_p = _rope_tables(jnp.arange(len_p, dtype=jnp.int32))
    tm_p = min(TOKEN_BLOCK, len_p)

    def attn_p(j, q, sg, kv):
        return _sb_prompt_call(q, kv[2], kv[3], sg, bias2[j], n_p)

    y_p, st_p, k_p, v_p = _trunk(x_prompt.astype(F32).reshape(n_p * len_p, d), mods_p, len_p // tm_p,
                                 n_p, rope_p, None, attn_p, weights, BF16, True)

    pos_s = past_len + jnp.arange(len_s, dtype=jnp.int32)
    rope_s = tuple(jnp.tile(t, (n_s, 1)) for t in _rope_tables(pos_s))

    n_pool = cache_k.shape[0]
    ck = cache_k.astype(BF16).reshape(n_pool, PAGE_SIZE, sb_w)
    cv = cache_v.astype(BF16).reshape(n_pool, PAGE_SIZE, sb_w)

    def attn_s(j, q, sg, kv):
        bias_lanes = jnp.repeat(bias2[j], len_s).reshape(1, SB_HEADS * len_s)
        return _sb_sample_call(q, kv[0], kv[1], sg, bias_lanes, ck, cv, page_table, len_s)

    y_s, st_s, k_s, v_s = _trunk(x_sample.astype(F32).reshape(n_s * len_s, d), mods_s, None,
                                 n_s, rope_s, state_ret.astype(F32), attn_s, weights, F32, False)

    return (y_p.reshape(n_p, len_p, d).astype(x_prompt.dtype),
            y_s.reshape(n_s, len_s, d).astype(x_sample.dtype),
            st_p, st_s,
            k_p.reshape(n_p, len_p, SB_HEADS, SB_DH), v_p.reshape(n_p, len_p, SB_HEADS, SB_DH),
            k_s.reshape(n_s, len_s, SB_HEADS, SB_DH), v_s.reshape(n_s, len_s, SB_HEADS, SB_DH))
```

```python
import functools
import math

import jax
import jax.numpy as jnp
from jax import lax
from jax.experimental import pallas as pl
from jax.experimental.pallas import tpu as pltpu

F32 = jnp.float32
BF16 = jnp.bfloat16

RET_HEADS = 8
RET_DK = 128
RET_DV = 256
RET_CHUNK = 128
SB_HEADS = 16
SB_DH = 64
PAGE_SIZE = 128
ROPE_BASE = 10000.0
EPS = 1e-6
LOG2E = 1.4426950408889634

LANES = 128
SUBLANES = 8
VMEM_LIMIT_BYTES = 56 * 1024 * 1024

TOKEN_BLOCK = 512
SB_KEY_BLOCK = 256
SB_QUERY_BLOCK = 512
SB_LANE_GROUP = 256
ADA_COL_BLOCK = 512
EXP2_CLAMP = 100.0


def _params(*sem):
    return pltpu.CompilerParams(dimension_semantics=sem, vmem_limit_bytes=VMEM_LIMIT_BYTES)


def _resident(a, n_grid):
    zeros = (0,) * a.ndim
    index_map = {1: lambda t: zeros, 2: lambda b, t: zeros}[n_grid]
    return pl.BlockSpec(a.shape, index_map, pipeline_mode=pl.Buffered(1))


def _split_bf16(a):
    hi = a.astype(BF16)
    lo = (a - hi.astype(F32)).astype(BF16)
    return hi, lo


def _dot(a, b):
    return jnp.dot(a, b, preferred_element_type=F32)


def _dot3(a, b):
    a_hi, a_lo = _split_bf16(a)
    b_hi, b_lo = _split_bf16(b)
    return _dot(a_hi, b_hi) + _dot(a_lo, b_hi) + _dot(a_hi, b_lo)


def _silu(x):
    return x / (1.0 + jnp.exp(-x))


def _ada_body(c_ref, w_ref, b_ref, o_ref):
    o_ref[...] = _dot3(_silu(c_ref[...]), w_ref[...]) + b_ref[...]


def _ada_call(c_pad, ada_w, ada_b):
    depth, d, d3 = ada_w.shape
    rows = c_pad.shape[0]
    tn = ADA_COL_BLOCK
    return pl.pallas_call(
        _ada_body,
        grid=(depth, d3 // tn),
        in_specs=[
            pl.BlockSpec((rows, d), lambda l, j: (0, 0)),
            pl.BlockSpec((None, d, tn), lambda l, j: (l, 0, j)),
            pl.BlockSpec((None, 1, tn), lambda l, j: (l, 0, j)),
        ],
        out_specs=pl.BlockSpec((None, rows, tn), lambda l, j: (l, 0, j)),
        out_shape=jax.ShapeDtypeStruct((depth, rows, d3), F32),
        compiler_params=_params("parallel", "parallel"),
        name="ada_mod",
    )(c_pad, ada_w, ada_b.reshape(depth, 1, d3))


def _modulated(x, g, m):
    d = x.shape[-1]
    r = lax.rsqrt(jnp.mean(x * x, axis=-1, keepdims=True) + EPS)
    return x * r * g * (1.0 + m[:, d:2 * d]) + m[:, :d]


def _mod_spec(mod, tm, blocks_per_seq):
    d3 = mod.shape[-1]
    if mod.shape[1] == 1:
        return pl.BlockSpec((None, 1, d3), lambda t: (t // blocks_per_seq, 0, 0))
    return pl.BlockSpec((None, tm, d3), lambda t: (0, t, 0))


def _tab_spec(tab, tm, blocks_per_seq):
    if blocks_per_seq is None:
        return pl.BlockSpec((tm, LANES), lambda t: (t, 0))
    return pl.BlockSpec((tm, LANES), lambda t: (t % blocks_per_seq, 0))


def _head_sum64(x_sq, ones_bd):
    hi, lo = _split_bf16(x_sq)
    return _dot(hi, ones_bd) + _dot(lo, ones_bd)


def _ret_proj_body(x_ref, m_ref, g_ref, w_ref, cos_ref, sin_ref, q_ref, k_ref, v_ref, sg_ref):
    h = _modulated(x_ref[...], g_ref[...], m_ref[...]).astype(BF16)
    cosf = cos_ref[...]
    sinf = sin_ref[...]
    qk_w = RET_HEADS * RET_DK
    v_w = RET_HEADS * RET_DV

    def rotary(p, scale):
        outs = []
        for hd in range(RET_HEADS):
            sl = p[:, hd * RET_DK:(hd + 1) * RET_DK]
            rot = pltpu.roll(sl, RET_DK // 2, axis=1)
            o = sl * cosf + rot * sinf
            outs.append(o if scale is None else o * scale)
        return jnp.concatenate(outs, axis=1)

    q = _dot(h, w_ref[:, 0:qk_w])
    q_ref[...] = rotary(q, None).astype(q_ref.dtype)
    k = _dot(h, w_ref[:, qk_w:2 * qk_w])
    k_ref[...] = rotary(k, RET_DK ** -0.5).astype(k_ref.dtype)
    v_ref[...] = _dot(h, w_ref[:, 2 * qk_w:2 * qk_w + v_w]).astype(v_ref.dtype)
    g = _dot(h, w_ref[:, 2 * qk_w + v_w:2 * qk_w + 2 * v_w])
    sg_ref[...] = _silu(g).astype(sg_ref.dtype)


def _ret_proj_call(x, mod, g, w_bf16, cosf, sinf, blocks_per_seq, out_dtype):
    t_total, d = x.shape
    tm = min(TOKEN_BLOCK, t_total)
    qk_w = RET_HEADS * RET_DK
    v_w = RET_HEADS * RET_DV
    row = lambda w: pl.BlockSpec((tm, w), lambda t: (t, 0))
    return pl.pallas_call(
        _ret_proj_body,
        grid=(t_total // tm,),
        in_specs=[
            row(d),
            _mod_spec(mod, tm, blocks_per_seq),
            pl.BlockSpec((1, d), lambda t: (0, 0)),
            _resident(w_bf16, 1),
            _tab_spec(cosf, tm, blocks_per_seq if mod.shape[1] == 1 else None),
            _tab_spec(sinf, tm, blocks_per_seq if mod.shape[1] == 1 else None),
        ],
        out_specs=[row(qk_w), row(qk_w), row(v_w), row(v_w)],
        out_shape=[
            jax.ShapeDtypeStruct((t_total, qk_w), out_dtype),
            jax.ShapeDtypeStruct((t_total, qk_w), out_dtype),
            jax.ShapeDtypeStruct((t_total, v_w), out_dtype),
            jax.ShapeDtypeStruct((t_total, v_w), out_dtype),
        ],
        compiler_params=_params("parallel"),
        name="ret_proj",
    )(x, mod, g.reshape(1, d), w_bf16, cosf, sinf)


def _ret_core_body(*refs, chunk, chunk_pad, n_chunks, has_s0, has_stack):
    q_ref, k_ref, v_ref, sg_ref, dm_ref, qd_ref, kd_ref, cd_ref, ng_ref = refs[:9]
    s0_ref = refs[9] if has_s0 else None
    og_ref, s_ref = refs[9 + has_s0 + has_stack:]
    mxu = BF16 if chunk >= 16 else F32

    @pl.when(pl.program_id(1) == 0)
    def _():
        if has_s0:
            s_ref[...] = s0_ref[...]
        else:
            s_ref[...] = jnp.zeros_like(s_ref)

    def pad_rows(a):
        if chunk_pad == chunk:
            return a
        return jnp.concatenate([a, jnp.zeros((chunk_pad - chunk, a.shape[1]), a.dtype)], axis=0)

    def one_chunk(ci, carry):
        r0 = pl.multiple_of(ci * chunk, chunk)
        rows = pl.ds(r0, chunk)
        for hd in range(RET_HEADS):
            qc = q_ref[rows, hd * RET_DK:(hd + 1) * RET_DK]
            kc = pad_rows(k_ref[rows, hd * RET_DK:(hd + 1) * RET_DK])
            vc = pad_rows(v_ref[rows, hd * RET_DV:(hd + 1) * RET_DV]).astype(mxu)
            s_old = s_ref[hd]
            scores = lax.dot_general(qc.astype(mxu), kc.astype(mxu), (((1,), (1,)), ((), ())),
                                     preferred_element_type=F32) * dm_ref[hd]
            intra = _dot(scores.astype(mxu), vc)
            qdec = (qc.astype(F32) * qd_ref[hd]).astype(mxu)
            cross = _dot(qdec, s_old.astype(mxu))
            o = intra + cross
            kdec = (kc.astype(F32) * kd_ref[hd]).astype(mxu)
            s_ref[hd] = s_old * cd_ref[hd] + lax.dot_general(
                kdec, vc, (((0,), (0,)), ((), ())), preferred_element_type=F32)
            r = lax.rsqrt(jnp.mean(o * o, axis=-1, keepdims=True) + EPS)
            cols = slice(hd * RET_DV, (hd + 1) * RET_DV)
            og = o * r * ng_ref[:, cols] * sg_ref[rows, cols].astype(F32)
            og_ref[rows, cols] = og.astype(og_ref.dtype)
        return carry

    if n_chunks == 1:
        one_chunk(0, 0)
    else:
        lax.fori_loop(0, n_chunks, one_chunk, 0)


def _ret_core_call(q, k, v, sg, tabs, norm_g, s0, layer, n_layers, states, n_seq, out_dtype):
    t_total = q.shape[0]
    seq_len = t_total // n_seq
    chunk = math.gcd(seq_len, RET_CHUNK)
    chunk_pad = max(chunk, LANES)
    tm = min(TOKEN_BLOCK, seq_len)
    n_chunks = tm // chunk
    nblk = seq_len // tm
    dm, qd, kd, cd = tabs
    qk_w = RET_HEADS * RET_DK
    v_w = RET_HEADS * RET_DV
    row = lambda w: pl.BlockSpec((tm, w), lambda b, t: (b * nblk + t, 0))
    full = lambda a: _resident(a, 2)
    state_spec = pl.BlockSpec((None, None, RET_HEADS, RET_DK, RET_DV),
                              lambda b, t: (layer, b, 0, 0, 0))
    in_specs = [row(qk_w), row(qk_w), row(v_w), row(v_w), full(dm), full(qd), full(kd), full(cd),
                pl.BlockSpec((1, v_w), lambda b, t: (0, 0))]
    args = [q, k, v, sg, dm, qd, kd, cd, norm_g.reshape(1, v_w)]
    if s0 is not None:
        in_specs.append(state_spec)
        args.append(s0)
    aliases = {}
    if states is not None:
        aliases = {len(args): 1}
        in_specs.append(pl.BlockSpec(memory_space=pl.ANY))
        args.append(states)
    body = functools.partial(_ret_core_body, chunk=chunk, chunk_pad=chunk_pad, n_chunks=n_chunks,
                             has_s0=s0 is not None, has_stack=states is not None)
    return pl.pallas_call(
        body,
        grid=(n_seq, nblk),
        in_specs=in_specs,
        out_specs=[row(v_w), state_spec],
        out_shape=[
            jax.ShapeDtypeStruct((t_total, v_w), out_dtype),
            jax.ShapeDtypeStruct((n_layers, n_seq, RET_HEADS, RET_DK, RET_DV), F32),
        ],
        input_output_aliases=aliases,
        compiler_params=_params("parallel", "arbitrary"),
        name="ret_core",
    )(*args)


def _ret_tables(chunk, chunk_pad):
    hds = jnp.arange(RET_HEADS, dtype=F32)
    log_gamma = jnp.log1p(-jnp.exp2(-5.0 - hds))
    idx = jnp.arange(chunk, dtype=F32)
    diff = idx[:, None] - idx[None, :]
    dmask = jnp.where(diff[None] >= 0,
                      jnp.exp(log_gamma[:, None, None] * jnp.maximum(diff, 0.0)[None]), 0.0)
    dmask = jnp.pad(dmask, ((0, 0), (0, 0), (0, chunk_pad - chunk)))
    q_dec = jnp.exp(log_gamma[:, None] * (idx[None, :] + 1.0))
    k_dec = jnp.exp(log_gamma[:, None] * (chunk - 1.0 - idx[None, :]))
    k_dec = jnp.pad(k_dec, ((0, 0), (0, chunk_pad - chunk)))
    c_dec = jnp.exp(log_gamma * chunk)
    qd = jnp.broadcast_to(q_dec[:, :, None], (RET_HEADS, chunk, RET_DK))
    kd = jnp.broadcast_to(k_dec[:, :, None], (RET_HEADS, chunk_pad, RET_DK))
    cd = jnp.broadcast_to(c_dec[:, None, None], (RET_HEADS, 1, RET_DV))
    return dmask.astype(F32), qd.astype(F32), kd.astype(F32), cd.astype(F32)


def _out_proj_body(a_ref, w_ref, x_ref, m_ref, o_ref):
    d = x_ref.shape[-1]
    y = _dot(a_ref[...].astype(BF16), w_ref[...])
    o_ref[...] = x_ref[...] + m_ref[:, 2 * d:3 * d] * y


def _out_proj_call(a, w_bf16, x, mod, blocks_per_seq):
    t_total, d = x.shape
    tm = min(TOKEN_BLOCK, t_total)
    ka = a.shape[1]
    return pl.pallas_call(
        _out_proj_body,
        grid=(t_total // tm,),
        in_specs=[
            pl.BlockSpec((tm, ka), lambda t: (t, 0)),
            _resident(w_bf16, 1),
            pl.BlockSpec((tm, d), lambda t: (t, 0)),
            _mod_spec(mod, tm, blocks_per_seq),
        ],
        out_specs=pl.BlockSpec((tm, d), lambda t: (t, 0)),
        out_shape=jax.ShapeDtypeStruct((t_total, d), F32),
        compiler_params=_params("parallel"),
        name="out_proj",
    )(a, w_bf16, x, mod)


def _kv_body(*refs, permuted):
    if permuted:
        x_ref, g_ref, w_ref, kg_ref, ones_ref, perm_ref, k_ref, v_ref, kp_ref, vt_ref = refs
    else:
        x_ref, g_ref, w_ref, kg_ref, ones_ref, k_ref, v_ref = refs
    x = x_ref[...]
    sb_w = SB_HEADS * SB_DH
    r = lax.rsqrt(jnp.mean(x * x, axis=-1, keepdims=True) + EPS)
    xn = (x * r * g_ref[...]).astype(BF16)
    k = _dot(xn, w_ref[:, :sb_w])
    v = _dot(xn, w_ref[:, sb_w:])
    ms = _head_sum64(k * k, ones_ref[...]) * (1.0 / SB_DH)
    k = k * lax.rsqrt(ms + EPS) * kg_ref[...]
    k_ref[...] = k
    v_ref[...] = v
    if permuted:
        kb = SB_KEY_BLOCK
        sup = SB_QUERY_BLOCK
        perm = perm_ref[...]
        for s in range(x.shape[0] // sup):
            cols = []
            for j in range(sup // kb):
                rows = slice(s * sup + j * kb, s * sup + (j + 1) * kb)
                kp_ref[rows, :] = _dot(perm, k[rows].astype(BF16)).astype(BF16)
                cols.append(_dot(perm, v[rows].astype(BF16)).T.astype(BF16))
            vt_ref[s] = jnp.concatenate(cols, axis=1)


def _kv_call(x, g, w_bf16, k_gain_row, ones_bd, perm, n_seq):
    t_total, d = x.shape
    tm = min(TOKEN_BLOCK, t_total)
    sb_w = SB_HEADS * SB_DH
    row = lambda w: pl.BlockSpec((tm, w), lambda t: (t, 0))
    const = lambda a: _resident(a, 1)
    in_specs = [row(d), pl.BlockSpec((1, d), lambda t: (0, 0)), const(w_bf16),
                const(k_gain_row), const(ones_bd)]
    args = [x, g.reshape(1, d), w_bf16, k_gain_row, ones_bd]
    out_specs = [row(sb_w), row(sb_w)]
    out_shape = [jax.ShapeDtypeStruct((t_total, sb_w), F32), jax.ShapeDtypeStruct((t_total, sb_w), F32)]
    if perm is not None:
        sup = SB_QUERY_BLOCK
        assert tm % sup == 0 and sup % SB_KEY_BLOCK == 0
        in_specs.append(const(perm))
        args.append(perm)
        out_specs += [row(sb_w), pl.BlockSpec((tm // sup, sb_w, sup), lambda t: (t, 0, 0))]
        out_shape += [jax.ShapeDtypeStruct((t_total, sb_w), BF16),
                      jax.ShapeDtypeStruct((t_total // sup, sb_w, sup), BF16)]
    return pl.pallas_call(
        functools.partial(_kv_body, permuted=perm is not None),
        grid=(t_total // tm,),
        in_specs=in_specs,
        out_specs=out_specs,
        out_shape=out_shape,
        compiler_params=_params("parallel"),
        name="shared_kv",
    )(*args)


def _sb_proj_body(x_ref, m_ref, g_ref, w_ref, qg_ref, ones_ref, q_ref, sg_ref):
    h = _modulated(x_ref[...], g_ref[...], m_ref[...]).astype(BF16)
    sb_w = SB_HEADS * SB_DH
    q = _dot(h, w_ref[:, :sb_w])
    ms = _head_sum64(q * q, ones_ref[...]) * (1.0 / SB_DH)
    q_ref[...] = (q * lax.rsqrt(ms + EPS) * qg_ref[...]).astype(q_ref.dtype)
    sg_ref[...] = _silu(_dot(h, w_ref[:, sb_w:])).astype(sg_ref.dtype)


def _sb_proj_call(x, mod, g, w_bf16, q_gain_row, ones_bd, blocks_per_seq, out_dtype):
    t_total, d = x.shape
    tm = min(TOKEN_BLOCK, t_total)
    sb_w = SB_HEADS * SB_DH
    row = lambda w: pl.BlockSpec((tm, w), lambda t: (t, 0))
    const = lambda a: _resident(a, 1)
    return pl.pallas_call(
        _sb_proj_body,
        grid=(t_total // tm,),
        in_specs=[row(d), _mod_spec(mod, tm, blocks_per_seq), pl.BlockSpec((1, d), lambda t: (0, 0)),
                  const(w_bf16), const(q_gain_row), const(ones_bd)],
        out_specs=[row(sb_w), row(sb_w)],
        out_shape=[jax.ShapeDtypeStruct((t_total, sb_w), out_dtype),
                   jax.ShapeDtypeStruct((t_total, sb_w), out_dtype)],
        compiler_params=_params("parallel"),
        name="sb_proj",
    )(x, mod, g.reshape(1, d), w_bf16, q_gain_row, ones_bd)


def _sb_blocks(zs, carry, masks):
    kb, nq = zs[0].shape
    n = kb // SUBLANES
    es = []
    for z_t, mask in zip(zs, masks):
        e = jnp.exp2(jnp.minimum(z_t, EXP2_CLAMP))
        es.append(e if mask is None else jnp.where(mask, e, 0.0))
    runs = [None] * len(zs)
    partial = [[None] * n for _ in zs]
    for i in range(n):
        for b, e in enumerate(es):
            a_i = 1.0 + e[i * SUBLANES:(i + 1) * SUBLANES, :]
            runs[b] = a_i if runs[b] is None else runs[b] * a_i
            partial[b][i] = runs[b]
    sub = lax.broadcasted_iota(jnp.int32, (SUBLANES, nq), 0)
    ws = []
    for b, e in enumerate(es):
        inc = runs[b]
        for step in (1, 2, 4):
            shifted = pltpu.roll(inc, SUBLANES - step, axis=0)
            inc = inc * jnp.where(sub < SUBLANES - step, shifted, 1.0)
        after = jnp.where(sub < SUBLANES - 1, pltpu.roll(inc, SUBLANES - 1, axis=0), 1.0)
        scale = carry * after
        carry = carry * jnp.broadcast_to(inc[0:1, :], (SUBLANES, nq))
        ws.append(jnp.concatenate(
            [e[i * SUBLANES:(i + 1) * SUBLANES, :] / (partial[b][i] * scale) for i in range(n)], axis=0))
    return ws, carry


def _sb_block(z_t, carry, mask):
    ws, carry = _sb_blocks([z_t], carry, [mask])
    return ws[0], carry


def _sb_key_offsets(kb, nq):
    row = lax.broadcasted_iota(jnp.int32, (kb, nq), 0)
    n = kb // SUBLANES
    return (row & (SUBLANES - 1)) * n + (n - 1 - (row >> 3))


def _sb_prompt_body(bias_ref, q_ref, kp_ref, vt_ref, sg_ref, og_ref, acc_ref):
    hp = pl.program_id(1)
    qi = pl.program_id(2)
    qb = q_ref.shape[0]
    kb = SB_KEY_BLOCK
    ratio = qb // kb
    q = q_ref[...]
    lane = lax.broadcasted_iota(jnp.int32, q.shape, 1)
    qms = [jnp.where((lane >= j * SB_DH) & (lane < (j + 1) * SB_DH), q, jnp.zeros_like(q))
           for j in range(2)]
    lg = SB_LANE_GROUP
    units = [(j, h) for j in range(2) for h in range(qb // lg)]
    q_unit = [qms[j][h * lg:(h + 1) * lg, :] for j, h in units]
    biases = [bias_ref[2 * hp + j] for j in range(2)]
    order = list(range(ratio - 1, -1, -1))

    def logits(ks, u):
        r0 = pl.multiple_of(ks * qb, qb)
        return lax.dot_general(kp_ref[pl.ds(r0, qb), :], q_unit[u], (((1,), (1,)), ((), ())),
                               preferred_element_type=F32) + biases[units[u][0]]

    def weights(z_t, carry, masks):
        live = [c for c in order if masks is None or masks[c] is not None]
        ws, carry = _sb_blocks([z_t[c * kb:(c + 1) * kb, :] for c in live], carry,
                               [None if masks is None else masks[c] for c in live])
        by_block = dict(zip(live, ws))
        rows = [by_block[c] if c in by_block else jnp.zeros((kb, lg), F32) for c in range(ratio)]
        return jnp.concatenate(rows, axis=0).astype(BF16), carry

    def accumulate(ks, u, w):
        j, h = units[u]
        acc_ref[j, :, h * lg:(h + 1) * lg] += _dot(vt_ref[ks, j * SB_DH:(j + 1) * SB_DH, :], w)

    def super_block(ks, ks_next, z_first, w_last, carries, masks):
        z_t, w_prev = z_first, w_last
        out = []
        for u in range(len(units)):
            z_ahead = logits(ks, u + 1) if u + 1 < len(units) else logits(ks_next, 0)
            if u > 0:
                accumulate(ks, u - 1, w_prev)
            elif w_prev is not None:
                accumulate(ks + 1, len(units) - 1, w_prev)
            w_prev, carry = weights(z_t, carries[u], None if masks is None else masks[units[u][1]])
            out.append(carry)
            z_t = z_ahead
        return z_t, w_prev, tuple(out)

    acc_ref[...] = jnp.zeros_like(acc_ref)
    key_off = _sb_key_offsets(kb, lg)
    q_idx = lax.broadcasted_iota(jnp.int32, (kb, lg), 1)
    masks = [[key_off + c * kb < q_idx + h * lg if c * kb < (h + 1) * lg - 1 else None
              for c in range(ratio)] for h in range(qb // lg)]
    ones = jnp.ones((SUBLANES, lg), F32)
    state = super_block(qi, jnp.maximum(qi - 1, 0), logits(qi, 0), None, (ones,) * len(units), masks)

    def trip(it, state):
        ks = qi - 1 - it
        return super_block(ks, jnp.maximum(ks - 1, 0), state[0], state[1], state[2], None)

    _, w_last, _ = lax.fori_loop(0, qi, trip, state)
    accumulate(0, len(units) - 1, w_last)
    o = jnp.concatenate([acc_ref[0], acc_ref[1]], axis=0).T
    og_ref[...] = (o * sg_ref[...].astype(F32)).astype(og_ref.dtype)


def _sb_prompt_call(q, kp, vt, sg, bias2, n_seq):
    t_total, sb_w = q.shape
    seq_len = t_total // n_seq
    qb = SB_QUERY_BLOCK
    assert seq_len % qb == 0
    n_q = seq_len // qb
    n_hp = sb_w // LANES
    grid_spec = pltpu.PrefetchScalarGridSpec(
        num_scalar_prefetch=1,
        grid=(n_seq, n_hp, n_q),
        in_specs=[
            pl.BlockSpec((qb, LANES), lambda b, hp, qi, bias: (b * n_q + qi, hp)),
            pl.BlockSpec((seq_len, LANES), lambda b, hp, qi, bias: (b, hp)),
            pl.BlockSpec((n_q, LANES, qb), lambda b, hp, qi, bias: (b, hp, 0)),
            pl.BlockSpec((qb, LANES), lambda b, hp, qi, bias: (b * n_q + qi, hp)),
        ],
        out_specs=pl.BlockSpec((qb, LANES), lambda b, hp, qi, bias: (b * n_q + qi, hp)),
        scratch_shapes=[pltpu.VMEM((2, SB_DH, qb), F32)],
    )
    return pl.pallas_call(
        _sb_prompt_body,
        grid_spec=grid_spec,
        out_shape=jax.ShapeDtypeStruct((t_total, sb_w), BF16),
        compiler_params=_params("parallel", "parallel", "arbitrary"),
        name="sb_attn_prompt",
    )(bias2, q, kp, vt, sg)


def _sb_sample_body(pt_ref, q_ref, kn_ref, vn_ref, sg_ref, bias_ref, *rest, n_pages, n_new):
    k_pages = rest[:n_pages]
    v_pages = rest[n_pages:2 * n_pages]
    og_ref = rest[2 * n_pages]
    kb_ref, vb_ref, z_ref, inc_ref, w_ref = rest[2 * n_pages + 1:]
    del pt_ref
    past = n_pages * PAGE_SIZE
    sb_w = SB_HEADS * SB_DH
    n_lane = SB_HEADS * n_new
    pad_rows = PAGE_SIZE - n_new

    for p in range(n_pages):
        kb_ref[p * PAGE_SIZE:(p + 1) * PAGE_SIZE, :] = k_pages[p][...].astype(BF16)
        vb_ref[p * PAGE_SIZE:(p + 1) * PAGE_SIZE, :] = v_pages[p][...].astype(BF16)
    zeros_pad = jnp.zeros((pad_rows, sb_w), F32)
    kb_ref[past:past + PAGE_SIZE, :] = jnp.concatenate([kn_ref[...], zeros_pad], axis=0).astype(BF16)
    vb_ref[past:past + PAGE_SIZE, :] = jnp.concatenate([vn_ref[...], zeros_pad], axis=0).astype(BF16)

    q = q_ref[...]
    q_rows = jnp.concatenate([q] * SB_HEADS, axis=0)
    row_head = lax.broadcasted_iota(jnp.int32, (n_lane, sb_w), 0) // n_new
    lane_head = lax.broadcasted_iota(jnp.int32, (n_lane, sb_w), 1) // SB_DH
    head_mask = row_head == lane_head
    q_bd = jnp.where(head_mask, q_rows, 0.0).T.astype(BF16)

    half = (n_pages // 2) * PAGE_SIZE
    z_ref[0:half, :] = _dot(kb_ref[0:half, :], q_bd) + bias_ref[...]
    z_ref[half:, :] = _dot(kb_ref[half:, :], q_bd) + bias_ref[...]
    w_ref[past + n_new:past + PAGE_SIZE, :] = jnp.zeros((pad_rows, n_lane), F32)

    key_idx = lax.broadcasted_iota(jnp.int32, (n_new, n_lane), 0)
    q_idx = lax.broadcasted_iota(jnp.int32, (n_new, n_lane), 1) % n_new
    carry = jnp.ones((SUBLANES, n_lane), F32)
    w_new, carry = _sb_block(z_ref[past:past + n_new, :], carry, key_idx < q_idx)
    w_ref[past:past + n_new, :] = w_new

    sub = lax.broadcasted_iota(jnp.int32, (SUBLANES, n_lane), 0)

    def local(it, _):
        rows = pl.ds(pl.multiple_of(it * SUBLANES, SUBLANES), SUBLANES)
        e = jnp.exp2(jnp.minimum(z_ref[rows, :], EXP2_CLAMP))
        inc = 1.0 + e
        for step in (1, 2, 4):
            inc = inc * jnp.where(sub < SUBLANES - step, pltpu.roll(inc, SUBLANES - step, axis=0), 1.0)
        z_ref[rows, :] = e
        inc_ref[rows, :] = inc
        return 0

    lax.fori_loop(0, past // SUBLANES, local, 0, unroll=8)

    def scan(it, carry):
        rows = pl.ds(pl.multiple_of(past - SUBLANES * (it + 1), SUBLANES), SUBLANES)
        inc = inc_ref[rows, :]
        w_ref[rows, :] = z_ref[rows, :] / (inc * carry)
        return carry * jnp.broadcast_to(inc[0:1, :], (SUBLANES, n_lane))

    lax.fori_loop(0, past // SUBLANES, scan, carry, unroll=8)

    w_t = w_ref[...].T.astype(BF16)
    out = jnp.concatenate([_dot(w_t, vb_ref[:, 0:sb_w // 2]), _dot(w_t, vb_ref[:, sb_w // 2:])],
                          axis=1)
    out = jnp.where(head_mask, out, 0.0)
    o = out[0:n_new, :]
    for hd in range(1, SB_HEADS):
        o = o + out[hd * n_new:(hd + 1) * n_new, :]
    og_ref[...] = o * sg_ref[...]


def _sb_sample_call(q, k_new, v_new, sg, bias_lanes, ck, cv, page_table, n_new):
    t_total, sb_w = q.shape
    n_seq, n_pages = page_table.shape
    assert n_new == SUBLANES and SB_HEADS * n_new == LANES
    n_lane = SB_HEADS * n_new
    rows_pad = (n_pages + 1) * PAGE_SIZE
    row = pl.BlockSpec((n_new, sb_w), lambda b, pt: (b, 0))

    def page_spec(p):
        return pl.BlockSpec((None, PAGE_SIZE, sb_w), lambda b, pt: (pt[b, p], 0, 0))

    grid_spec = pltpu.PrefetchScalarGridSpec(
        num_scalar_prefetch=1,
        grid=(n_seq,),
        in_specs=[row, row, row, row, pl.BlockSpec((1, n_lane), lambda b, pt: (0, 0))]
        + [page_spec(p) for p in range(n_pages)] + [page_spec(p) for p in range(n_pages)],
        out_specs=row,
        scratch_shapes=[
            pltpu.VMEM((rows_pad, sb_w), BF16),
            pltpu.VMEM((rows_pad, sb_w), BF16),
            pltpu.VMEM((rows_pad, n_lane), F32),
            pltpu.VMEM((n_pages * PAGE_SIZE, n_lane), F32),
            pltpu.VMEM((rows_pad, n_lane), F32),
        ],
    )
    return pl.pallas_call(
        functools.partial(_sb_sample_body, n_pages=n_pages, n_new=n_new),
        grid_spec=grid_spec,
        out_shape=jax.ShapeDtypeStruct((t_total, sb_w), F32),
        compiler_params=_params("arbitrary"),
        name="sb_attn_sample",
    )(page_table, q, k_new, v_new, sg, bias_lanes, *([ck] * n_pages), *([cv] * n_pages))


def _rope_tables(pos):
    half = RET_DK // 2
    inv_freq = ROPE_BASE ** (-jnp.arange(half, dtype=F32) / half)
    ang = pos.astype(F32)[:, None] * inv_freq[None, :]
    cos, sin = jnp.cos(ang), jnp.sin(ang)
    return jnp.concatenate([cos, cos], axis=1), jnp.concatenate([-sin, sin], axis=1)


def _sb_perm_matrix(kb):
    row = jnp.arange(kb)
    n = kb // SUBLANES
    src = (row % SUBLANES) * n + (n - 1 - row // SUBLANES)
    return (src[:, None] == jnp.arange(kb)[None, :]).astype(BF16)


def _trunk(x, mods, blocks_per_seq, n_seq, rope, s0, attn_fn, w, act_dtype, permuted):
    (norm_g, ret_w_in, ret_norm_g, ret_w_out, kv_norm_g, w_kv, sb_q_g, sb_k_g,
     sb_w_in, sb_w_out, ones_bd) = w
    n_a = ret_w_in.shape[0]
    n_b = sb_w_in.shape[0]
    seq_len = x.shape[0] // n_seq
    chunk = math.gcd(seq_len, RET_CHUNK)
    tabs = _ret_tables(chunk, max(chunk, LANES))
    states = None
    for l in range(n_a):
        q, k, v, sg = _ret_proj_call(x, mods[l], norm_g[l], ret_w_in[l], rope[0], rope[1],
                                     blocks_per_seq, act_dtype)
        og, states = _ret_core_call(q, k, v, sg, tabs, ret_norm_g[l], s0, l, n_a, states,
                                    n_seq, act_dtype)
        x = _out_proj_call(og, ret_w_out[l], x, mods[l], blocks_per_seq)
    k_gain_row = jnp.tile(sb_k_g.astype(F32), SB_HEADS).reshape(1, -1)
    perm = _sb_perm_matrix(SB_KEY_BLOCK) if permuted else None
    kv = _kv_call(x, kv_norm_g, w_kv, k_gain_row, ones_bd, perm, n_seq)
    for j in range(n_b):
        l = n_a + j
        q_gain_row = jnp.tile(sb_q_g[j].astype(F32) * (SB_DH ** -0.5 * LOG2E), SB_HEADS).reshape(1, -1)
        q, sg = _sb_proj_call(x, mods[l], norm_g[l], sb_w_in[j], q_gain_row, ones_bd,
                              blocks_per_seq, act_dtype)
        og = attn_fn(j, q, sg, kv)
        x = _out_proj_call(og, sb_w_out[j], x, mods[l], blocks_per_seq)
    return x, states, kv[0], kv[1]


def kernel(x_prompt, x_sample, state_ret, cache_k, cache_v, page_table, c_prompt, c_sample,
           ada_w, ada_b, norm_g, ret_w_in, ret_norm_g, ret_w_out,
           kv_norm_g, w_kv, sb_q_g, sb_k_g, sb_w_in, sb_w_out, sb_bias):
    n_p, len_p, d = x_prompt.shape
    n_s, len_s, _ = x_sample.shape
    depth = ada_w.shape[0]
    sb_w = SB_HEADS * SB_DH
    past_len = page_table.shape[1] * PAGE_SIZE

    n_c = n_p + n_s
    c_rows = -(-n_c // 16) * 16
    c_all = jnp.concatenate([c_prompt.astype(F32), c_sample.astype(F32),
                             jnp.zeros((c_rows - n_c, d), F32)], axis=0)
    mod = _ada_call(c_all, ada_w.astype(F32), ada_b.astype(F32))
    mods_p = [mod[l, :n_p].reshape(n_p, 1, 3 * d) for l in range(depth)]
    mods_s = [jnp.repeat(mod[l, n_p:n_c], len_s, axis=0).reshape(1, n_s * len_s, 3 * d)
              for l in range(depth)]

    ones_bd = (jnp.arange(sb_w)[:, None] // SB_DH == jnp.arange(sb_w)[None, :] // SB_DH).astype(BF16)
    weights = (norm_g.astype(F32), ret_w_in.astype(BF16), ret_norm_g.astype(F32), ret_w_out.astype(BF16),
               kv_norm_g.astype(F32), w_kv.astype(BF16), sb_q_g, sb_k_g,
               sb_w_in.astype(BF16), sb_w_out.astype(BF16), ones_bd)
    bias2 = sb_bias.astype(F32) * LOG2E

    rope_p = _rope_tables(jnp.arange(len_p, dtype=jnp.int32))
    tm_p = min(TOKEN_BLOCK, len_p)

    def attn_p(j, q, sg, kv):
        return _sb_prompt_call(q, kv[2], kv[3], sg, bias2[j], n_p)

    y_p, st_p, k_p, v_p = _trunk(x_prompt.astype(F32).reshape(n_p * len_p, d), mods_p, len_p // tm_p,
                                 n_p, rope_p, None, attn_p, weights, BF16, True)

    pos_s = past_len + jnp.arange(len_s, dtype=jnp.int32)
    rope_s = tuple(jnp.tile(t, (n_s, 1)) for t in _rope_tables(pos_s))

    n_pool = cache_k.shape[0]
    ck = cache_k.astype(BF16).reshape(n_pool, PAGE_SIZE, sb_w)
    cv = cache_v.astype(BF16).reshape(n_pool, PAGE_SIZE, sb_w)

    def attn_s(j, q, sg, kv):
        bias_lanes = jnp.repeat(bias2[j], len_s).reshape(1, SB_HEADS * len_s)
        return _sb_sample_call(q, kv[0], kv[1], sg, bias_lanes, ck, cv, page_table, len_s)

    y_s, st_s, k_s, v_s = _trunk(x_sample.astype(F32).reshape(n_s * len_s, d), mods_s, None,
                                 n_s, rope_s, state_ret.astype(F32), attn_s, weights, F32, False)

    return (y_p.reshape(n_p, len_p, d).astype(x_prompt.dtype),
            y_s.reshape(n_s, len_s, d).astype(x_sample.dtype),
            st_p, st_s,
            k_p.reshape(n_p, len_p, SB_HEADS, SB_DH), v_p.reshape(n_p, len_p, SB_HEADS, SB_DH),
            k_s.reshape(n_s, len_s, SB_HEADS, SB_DH), v_s.reshape(n_s, len_s, SB_HEADS, SB_DH))
```

```python
import functools
import math

import jax
import jax.numpy as jnp
from jax import lax
from jax.experimental import pallas as pl
from jax.experimental.pallas import tpu as pltpu

F32 = jnp.float32
BF16 = jnp.bfloat16

RET_HEADS = 8
RET_DK = 128
RET_DV = 256
RET_CHUNK = 128
SB_HEADS = 16
SB_DH = 64
PAGE_SIZE = 128
ROPE_BASE = 10000.0
EPS = 1e-6

LANES = 128
SUBLANES = 8
VMEM_LIMIT_BYTES = 56 * 1024 * 1024

TOKEN_BLOCK = 512
SB_KEY_BLOCK = 256
SB_QUERY_BLOCK = 512
SB_LANE_GROUP = 256
ADA_COL_BLOCK = 512


def _params(*sem):
    return pltpu.CompilerParams(dimension_semantics=sem, vmem_limit_bytes=VMEM_LIMIT_BYTES)


def _resident(a, n_grid):
    zeros = (0,) * a.ndim
    index_map = {1: lambda t: zeros, 2: lambda b, t: zeros}[n_grid]
    return pl.BlockSpec(a.shape, index_map, pipeline_mode=pl.Buffered(1))


def _split_bf16(a):
    hi = a.astype(BF16)
    lo = (a - hi.astype(F32)).astype(BF16)
    return hi, lo


def _dot(a, b):
    return jnp.dot(a, b, preferred_element_type=F32)


def _dot3(a, b):
    a_hi, a_lo = _split_bf16(a)
    b_hi, b_lo = _split_bf16(b)
    return _dot(a_hi, b_hi) + _dot(a_lo, b_hi) + _dot(a_hi, b_lo)


def _silu(x):
    return x / (1.0 + jnp.exp(-x))


def _ada_body(c_ref, w_ref, b_ref, o_ref):
    o_ref[...] = _dot3(_silu(c_ref[...]), w_ref[...]) + b_ref[...]


def _ada_call(c_pad, ada_w, ada_b):
    depth, d, d3 = ada_w.shape
    rows = c_pad.shape[0]
    tn = ADA_COL_BLOCK
    return pl.pallas_call(
        _ada_body,
        grid=(depth, d3 // tn),
        in_specs=[
            pl.BlockSpec((rows, d), lambda l, j: (0, 0)),
            pl.BlockSpec((None, d, tn), lambda l, j: (l, 0, j)),
            pl.BlockSpec((None, 1, tn), lambda l, j: (l, 0, j)),
        ],
        out_specs=pl.BlockSpec((None, rows, tn), lambda l, j: (l, 0, j)),
        out_shape=jax.ShapeDtypeStruct((depth, rows, d3), F32),
        compiler_params=_params("parallel", "parallel"),
        name="ada_mod",
    )(c_pad, ada_w, ada_b.reshape(depth, 1, d3))


def _modulated(x, g, m):
    d = x.shape[-1]
    r = lax.rsqrt(jnp.mean(x * x, axis=-1, keepdims=True) + EPS)
    return x * r * g * (1.0 + m[:, d:2 * d]) + m[:, :d]


def _mod_spec(mod, tm, blocks_per_seq):
    d3 = mod.shape[-1]
    if mod.shape[1] == 1:
        return pl.BlockSpec((None, 1, d3), lambda t: (t // blocks_per_seq, 0, 0))
    return pl.BlockSpec((None, tm, d3), lambda t: (0, t, 0))


def _tab_spec(tab, tm, blocks_per_seq):
    if blocks_per_seq is None:
        return pl.BlockSpec((tm, LANES), lambda t: (t, 0))
    return pl.BlockSpec((tm, LANES), lambda t: (t % blocks_per_seq, 0))


def _head_sum64(x_sq, ones_bd):
    hi, lo = _split_bf16(x_sq)
    return _dot(hi, ones_bd) + _dot(lo, ones_bd)


def _ret_proj_body(x_ref, m_ref, g_ref, w_ref, cos_ref, sin_ref, q_ref, k_ref, v_ref, sg_ref):
    h = _modulated(x_ref[...], g_ref[...], m_ref[...]).astype(BF16)
    cosf = cos_ref[...]
    sinf = sin_ref[...]
    qk_w = RET_HEADS * RET_DK
    v_w = RET_HEADS * RET_DV

    def rotary(p, scale):
        outs = []
        for hd in range(RET_HEADS):
            sl = p[:, hd * RET_DK:(hd + 1) * RET_DK]
            rot = pltpu.roll(sl, RET_DK // 2, axis=1)
            o = sl * cosf + rot * sinf
            outs.append(o if scale is None else o * scale)
        return jnp.concatenate(outs, axis=1)

    q = _dot(h, w_ref[:, 0:qk_w])
    q_ref[...] = rotary(q, None).astype(q_ref.dtype)
    k = _dot(h, w_ref[:, qk_w:2 * qk_w])
    k_ref[...] = rotary(k, RET_DK ** -0.5).astype(k_ref.dtype)
    v_ref[...] = _dot(h, w_ref[:, 2 * qk_w:2 * qk_w + v_w]).astype(v_ref.dtype)
    g = _dot(h, w_ref[:, 2 * qk_w + v_w:2 * qk_w + 2 * v_w])
    sg_ref[...] = _silu(g).astype(sg_ref.dtype)


def _ret_proj_call(x, mod, g, w_bf16, cosf, sinf, blocks_per_seq, out_dtype):
    t_total, d = x.shape
    tm = min(TOKEN_BLOCK, t_total)
    qk_w = RET_HEADS * RET_DK
    v_w = RET_HEADS * RET_DV
    row = lambda w: pl.BlockSpec((tm, w), lambda t: (t, 0))
    return pl.pallas_call(
        _ret_proj_body,
        grid=(t_total // tm,),
        in_specs=[
            row(d),
            _mod_spec(mod, tm, blocks_per_seq),
            pl.BlockSpec((1, d), lambda t: (0, 0)),
            _resident(w_bf16, 1),
            _tab_spec(cosf, tm, blocks_per_seq if mod.shape[1] == 1 else None),
            _tab_spec(sinf, tm, blocks_per_seq if mod.shape[1] == 1 else None),
        ],
        out_specs=[row(qk_w), row(qk_w), row(v_w), row(v_w)],
        out_shape=[
            jax.ShapeDtypeStruct((t_total, qk_w), out_dtype),
            jax.ShapeDtypeStruct((t_total, qk_w), out_dtype),
            jax.ShapeDtypeStruct((t_total, v_w), out_dtype),
            jax.ShapeDtypeStruct((t_total, v_w), out_dtype),
        ],
        compiler_params=_params("parallel"),
        name="ret_proj",
    )(x, mod, g.reshape(1, d), w_bf16, cosf, sinf)


def _ret_core_body(*refs, chunk, chunk_pad, n_chunks, has_s0, has_stack):
    q_ref, k_ref, v_ref, sg_ref, dm_ref, qd_ref, kd_ref, cd_ref, ng_ref = refs[:9]
    s0_ref = refs[9] if has_s0 else None
    og_ref, s_ref = refs[9 + has_s0 + has_stack:]
    mxu = BF16 if chunk >= 16 else F32

    @pl.when(pl.program_id(1) == 0)
    def _():
        if has_s0:
            s_ref[...] = s0_ref[...]
        else:
            s_ref[...] = jnp.zeros_like(s_ref)

    def pad_rows(a):
        if chunk_pad == chunk:
            return a
        return jnp.concatenate([a, jnp.zeros((chunk_pad - chunk, a.shape[1]), a.dtype)], axis=0)

    def one_chunk(ci, carry):
        r0 = pl.multiple_of(ci * chunk, chunk)
        rows = pl.ds(r0, chunk)
        for hd in range(RET_HEADS):
            qc = q_ref[rows, hd * RET_DK:(hd + 1) * RET_DK]
            kc = pad_rows(k_ref[rows, hd * RET_DK:(hd + 1) * RET_DK])
            vc = pad_rows(v_ref[rows, hd * RET_DV:(hd + 1) * RET_DV]).astype(mxu)
            s_old = s_ref[hd]
            scores = lax.dot_general(qc.astype(mxu), kc.astype(mxu), (((1,), (1,)), ((), ())),
                                     preferred_element_type=F32) * dm_ref[hd]
            intra = _dot(scores.astype(mxu), vc)
            qdec = (qc.astype(F32) * qd_ref[hd]).astype(mxu)
            cross = _dot(qdec, s_old.astype(mxu))
            o = intra + cross
            kdec = (kc.astype(F32) * kd_ref[hd]).astype(mxu)
            s_ref[hd] = s_old * cd_ref[hd] + lax.dot_general(
                kdec, vc, (((0,), (0,)), ((), ())), preferred_element_type=F32)
            r = lax.rsqrt(jnp.mean(o * o, axis=-1, keepdims=True) + EPS)
            cols = slice(hd * RET_DV, (hd + 1) * RET_DV)
            og = o * r * ng_ref[:, cols] * sg_ref[rows, cols].astype(F32)
            og_ref[rows, cols] = og.astype(og_ref.dtype)
        return carry

    if n_chunks == 1:
        one_chunk(0, 0)
    else:
        lax.fori_loop(0, n_chunks, one_chunk, 0)


def _ret_core_call(q, k, v, sg, tabs, norm_g, s0, layer, n_layers, states, n_seq, out_dtype):
    t_total = q.shape[0]
    seq_len = t_total // n_seq
    chunk = math.gcd(seq_len, RET_CHUNK)
    chunk_pad = max(chunk, LANES)
    tm = min(TOKEN_BLOCK, seq_len)
    n_chunks = tm // chunk
    nblk = seq_len // tm
    dm, qd, kd, cd = tabs
    qk_w = RET_HEADS * RET_DK
    v_w = RET_HEADS * RET_DV
    row = lambda w: pl.BlockSpec((tm, w), lambda b, t: (b * nblk + t, 0))
    full = lambda a: _resident(a, 2)
    state_spec = pl.BlockSpec((None, None, RET_HEADS, RET_DK, RET_DV),
                              lambda b, t: (layer, b, 0, 0, 0))
    in_specs = [row(qk_w), row(qk_w), row(v_w), row(v_w), full(dm), full(qd), full(kd), full(cd),
                pl.BlockSpec((1, v_w), lambda b, t: (0, 0))]
    args = [q, k, v, sg, dm, qd, kd, cd, norm_g.reshape(1, v_w)]
    if s0 is not None:
        in_specs.append(state_spec)
        args.append(s0)
    aliases = {}
    if states is not None:
        aliases = {len(args): 1}
        in_specs.append(pl.BlockSpec(memory_space=pl.ANY))
        args.append(states)
    body = functools.partial(_ret_core_body, chunk=chunk, chunk_pad=chunk_pad, n_chunks=n_chunks,
                             has_s0=s0 is not None, has_stack=states is not None)
    return pl.pallas_call(
        body,
        grid=(n_seq, nblk),
        in_specs=in_specs,
        out_specs=[row(v_w), state_spec],
        out_shape=[
            jax.ShapeDtypeStruct((t_total, v_w), out_dtype),
            jax.ShapeDtypeStruct((n_layers, n_seq, RET_HEADS, RET_DK, RET_DV), F32),
        ],
        input_output_aliases=aliases,
        compiler_params=_params("parallel", "arbitrary"),
        name="ret_core",
    )(*args)


def _ret_tables(chunk, chunk_pad):
    hds = jnp.arange(RET_HEADS, dtype=F32)
    log_gamma = jnp.log1p(-jnp.exp2(-5.0 - hds))
    idx = jnp.arange(chunk, dtype=F32)
    diff = idx[:, None] - idx[None, :]
    dmask = jnp.where(diff[None] >= 0,
                      jnp.exp(log_gamma[:, None, None] * jnp.maximum(diff, 0.0)[None]), 0.0)
    dmask = jnp.pad(dmask, ((0, 0), (0, 0), (0, chunk_pad - chunk)))
    q_dec = jnp.exp(log_gamma[:, None] * (idx[None, :] + 1.0))
    k_dec = jnp.exp(log_gamma[:, None] * (chunk - 1.0 - idx[None, :]))
    k_dec = jnp.pad(k_dec, ((0, 0), (0, chunk_pad - chunk)))
    c_dec = jnp.exp(log_gamma * chunk)
    qd = jnp.broadcast_to(q_dec[:, :, None], (RET_HEADS, chunk, RET_DK))
    kd = jnp.broadcast_to(k_dec[:, :, None], (RET_HEADS, chunk_pad, RET_DK))
    cd = jnp.broadcast_to(c_dec[:, None, None], (RET_HEADS, 1, RET_DV))
    return dmask.astype(F32), qd.astype(F32), kd.astype(F32), cd.astype(F32)


def _out_proj_body(a_ref, w_ref, x_ref, m_ref, o_ref):
    d = x_ref.shape[-1]
    y = _dot(a_ref[...].astype(BF16), w_ref[...])
    o_ref[...] = x_ref[...] + m_ref[:, 2 * d:3 * d] * y


def _out_proj_call(a, w_bf16, x, mod, blocks_per_seq):
    t_total, d = x.shape
    tm = min(TOKEN_BLOCK, t_total)
    ka = a.shape[1]
    return pl.pallas_call(
        _out_proj_body,
        grid=(t_total // tm,),
        in_specs=[
            pl.BlockSpec((tm, ka), lambda t: (t, 0)),
            _resident(w_bf16, 1),
            pl.BlockSpec((tm, d), lambda t: (t, 0)),
            _mod_spec(mod, tm, blocks_per_seq),
        ],
        out_specs=pl.BlockSpec((tm, d), lambda t: (t, 0)),
        out_shape=jax.ShapeDtypeStruct((t_total, d), F32),
        compiler_params=_params("parallel"),
        name="out_proj",
    )(a, w_bf16, x, mod)


def _kv_body(*refs, permuted):
    if permuted:
        (x_ref, g_ref, w_ref, kg_ref, ones_ref, perm_ref, slots_ref,
         k_ref, v_ref, kp_ref, vt_ref) = refs
    else:
        x_ref, g_ref, w_ref, kg_ref, ones_ref, k_ref, v_ref = refs
    x = x_ref[...]
    sb_w = SB_HEADS * SB_DH
    r = lax.rsqrt(jnp.mean(x * x, axis=-1, keepdims=True) + EPS)
    xn = (x * r * g_ref[...]).astype(BF16)
    k = _dot(xn, w_ref[:, :sb_w])
    v = _dot(xn, w_ref[:, sb_w:])
    ms = _head_sum64(k * k, ones_ref[...]) * (1.0 / SB_DH)
    k = k * lax.rsqrt(ms + EPS) * kg_ref[...]
    k_ref[...] = k
    v_ref[...] = v
    if permuted:
        kb = SB_KEY_BLOCK
        sup = SB_QUERY_BLOCK
        perm = perm_ref[...]
        for s in range(x.shape[0] // sup):
            cols = []
            for j in range(sup // kb):
                rows = slice(s * sup + j * kb, s * sup + (j + 1) * kb)
                kp_ref[rows, :] = _expand_heads(_dot(perm, k[rows].astype(BF16)),
                                                slots_ref[...]).astype(BF16)
                cols.append(_dot(perm, v[rows].astype(BF16)).T.astype(BF16))
            vt_ref[s] = jnp.concatenate(cols, axis=1)


def _kv_call(x, g, w_bf16, k_gain_row, ones_bd, perm, one_slots, n_seq):
    t_total, d = x.shape
    tm = min(TOKEN_BLOCK, t_total)
    sb_w = SB_HEADS * SB_DH
    row = lambda w: pl.BlockSpec((tm, w), lambda t: (t, 0))
    const = lambda a: _resident(a, 1)
    in_specs = [row(d), pl.BlockSpec((1, d), lambda t: (0, 0)), const(w_bf16),
                const(k_gain_row), const(ones_bd)]
    args = [x, g.reshape(1, d), w_bf16, k_gain_row, ones_bd]
    out_specs = [row(sb_w), row(sb_w)]
    out_shape = [jax.ShapeDtypeStruct((t_total, sb_w), F32), jax.ShapeDtypeStruct((t_total, sb_w), F32)]
    if perm is not None:
        sup = SB_QUERY_BLOCK
        assert tm % sup == 0 and sup % SB_KEY_BLOCK == 0
        kp_w = SB_HEADS * LANES
        in_specs += [const(perm), const(one_slots)]
        args += [perm, one_slots]
        out_specs += [row(kp_w), pl.BlockSpec((tm // sup, sb_w, sup), lambda t: (t, 0, 0))]
        out_shape += [jax.ShapeDtypeStruct((t_total, kp_w), BF16),
                      jax.ShapeDtypeStruct((t_total // sup, sb_w, sup), BF16)]
    return pl.pallas_call(
        functools.partial(_kv_body, permuted=perm is not None),
        grid=(t_total // tm,),
        in_specs=in_specs,
        out_specs=out_specs,
        out_shape=out_shape,
        compiler_params=_params("parallel"),
        name="shared_kv",
    )(*args)


def _expand_heads(x, slots):
    lane = lax.broadcasted_iota(jnp.int32, (x.shape[0], LANES), 1)
    cols = []
    for hd in range(SB_HEADS):
        col = x[:, (hd // 2) * LANES:(hd // 2 + 1) * LANES]
        if hd % 2:
            col = pltpu.roll(col, SB_DH, axis=1)
        cols.append(jnp.where(lane < SB_DH, col, slots[:, hd * LANES:(hd + 1) * LANES]))
    return jnp.concatenate(cols, axis=1)


def _sb_proj_body(*refs, expand):
    if expand:
        x_ref, m_ref, g_ref, w_ref, qg_ref, ones_ref, slots_ref, q_ref, sg_ref = refs
    else:
        x_ref, m_ref, g_ref, w_ref, qg_ref, ones_ref, q_ref, sg_ref = refs
    h = _modulated(x_ref[...], g_ref[...], m_ref[...]).astype(BF16)
    sb_w = SB_HEADS * SB_DH
    q = _dot(h, w_ref[:, :sb_w])
    ms = _head_sum64(q * q, ones_ref[...]) * (1.0 / SB_DH)
    q = q * lax.rsqrt(ms + EPS) * qg_ref[...]
    if expand:
        q = _expand_heads(q, slots_ref[...])
    q_ref[...] = q.astype(q_ref.dtype)
    sg_ref[...] = _silu(_dot(h, w_ref[:, sb_w:])).astype(sg_ref.dtype)


def _sb_proj_call(x, mod, g, w_bf16, q_gain_row, ones_bd, bias_slots, blocks_per_seq, out_dtype):
    t_total, d = x.shape
    tm = min(TOKEN_BLOCK, t_total)
    sb_w = SB_HEADS * SB_DH
    row = lambda w: pl.BlockSpec((tm, w), lambda t: (t, 0))
    const = lambda a: _resident(a, 1)
    expand = bias_slots is not None
    q_w = SB_HEADS * LANES if expand else sb_w
    in_specs = [row(d), _mod_spec(mod, tm, blocks_per_seq), pl.BlockSpec((1, d), lambda t: (0, 0)),
                const(w_bf16), const(q_gain_row), const(ones_bd)]
    args = [x, mod, g.reshape(1, d), w_bf16, q_gain_row, ones_bd]
    if expand:
        in_specs.append(const(bias_slots))
        args.append(bias_slots)
    return pl.pallas_call(
        functools.partial(_sb_proj_body, expand=expand),
        grid=(t_total // tm,),
        in_specs=in_specs,
        out_specs=[row(q_w), row(sb_w)],
        out_shape=[jax.ShapeDtypeStruct((t_total, q_w), out_dtype),
                   jax.ShapeDtypeStruct((t_total, sb_w), out_dtype)],
        compiler_params=_params("parallel"),
        name="sb_proj",
    )(*args)


def _sb_beta_keep(u, mask):
    half_t = 0.5 * jnp.tanh(u)
    keep, beta = 0.5 - half_t, 0.5 + half_t
    if mask is not None:
        keep, beta = jnp.where(mask, keep, 1.0), jnp.where(mask, beta, 0.0)
    return beta, keep


def _sublane_suffix_products(x):
    sub = lax.broadcasted_iota(jnp.int32, x.shape, 0)
    inc = x
    for step in (1, 2, 4):
        shifted = pltpu.roll(inc, SUBLANES - step, axis=0)
        inc = inc * jnp.where(sub < SUBLANES - step, shifted, 1.0)
    after = jnp.where(sub < SUBLANES - 1, pltpu.roll(inc, SUBLANES - 1, axis=0), 1.0)
    return inc, after


def _sb_blocks(us, carry, masks):
    kb, nq = us[0].shape
    n = kb // SUBLANES
    runs = [None] * len(us)
    half_t = [[None] * n for _ in us]
    nearer = [[None] * n for _ in us]
    for i in range(n):
        rows = slice(i * SUBLANES, (i + 1) * SUBLANES)
        for b, (u, mask) in enumerate(zip(us, masks)):
            h_i = 0.5 * jnp.tanh(u[rows, :])
            if mask is not None:
                h_i = jnp.where(mask[rows, :], h_i, -0.5)
            half_t[b][i] = h_i
            nearer[b][i] = runs[b]
            k_i = 0.5 - h_i
            runs[b] = k_i if runs[b] is None else runs[b] * k_i
    ws = []
    for b in range(len(us)):
        inc, after = _sublane_suffix_products(runs[b])
        scale = carry * after
        carry = carry * jnp.broadcast_to(inc[0:1, :], (SUBLANES, nq))
        ws.append(jnp.concatenate(
            [(0.5 + half_t[b][i]) * (scale if nearer[b][i] is None else nearer[b][i] * scale)
             for i in range(n)], axis=0))
    return ws, carry


def _sb_block(z_t, carry, mask):
    ws, carry = _sb_blocks([z_t], carry, [mask])
    return ws[0], carry


def _sb_key_offsets(kb, nq):
    row = lax.broadcasted_iota(jnp.int32, (kb, nq), 0)
    n = kb // SUBLANES
    return (row & (SUBLANES - 1)) * n + (n - 1 - (row >> 3))


def _sb_prompt_body(q_ref, kp_ref, vt_ref, sg_ref, og_ref, acc_ref):
    qi = pl.program_id(2)
    qb = q_ref.shape[0]
    kb = SB_KEY_BLOCK
    ratio = qb // kb
    lg = SB_LANE_GROUP
    units = [(j, h) for j in range(2) for h in range(qb // lg)]
    q_unit = [q_ref[h * lg:(h + 1) * lg, j * LANES:(j + 1) * LANES] for j, h in units]
    order = list(range(ratio - 1, -1, -1))

    def logits(ks, u):
        r0 = pl.multiple_of(ks * qb, qb)
        j = units[u][0]
        return lax.dot_general(kp_ref[pl.ds(r0, qb), j * LANES:(j + 1) * LANES], q_unit[u],
                               (((1,), (1,)), ((), ())), preferred_element_type=F32)

    def weights(z_t, carry, masks):
        live = [c for c in order if masks is None or masks[c] is not None]
        ws, carry = _sb_blocks([z_t[c * kb:(c + 1) * kb, :] for c in live], carry,
                               [None if masks is None else masks[c] for c in live])
        by_block = dict(zip(live, ws))
        rows = [by_block[c] if c in by_block else jnp.zeros((kb, lg), F32) for c in range(ratio)]
        return jnp.concatenate(rows, axis=0).astype(BF16), carry

    def accumulate(ks, u, w):
        j, h = units[u]
        acc_ref[j, :, h * lg:(h + 1) * lg] += _dot(vt_ref[ks, j * SB_DH:(j + 1) * SB_DH, :], w)

    def super_block(ks, ks_next, z_first, w_last, carries, masks):
        z_t, w_prev = z_first, w_last
        out = []
        for u in range(len(units)):
            z_ahead = logits(ks, u + 1) if u + 1 < len(units) else logits(ks_next, 0)
            if u > 0:
                accumulate(ks, u - 1, w_prev)
            elif w_prev is not None:
                accumulate(ks + 1, len(units) - 1, w_prev)
            w_prev, carry = weights(z_t, carries[u], None if masks is None else masks[units[u][1]])
            out.append(carry)
            z_t = z_ahead
        return z_t, w_prev, tuple(out)

    acc_ref[...] = jnp.zeros_like(acc_ref)
    key_off = _sb_key_offsets(kb, lg)
    q_idx = lax.broadcasted_iota(jnp.int32, (kb, lg), 1)
    masks = [[key_off + c * kb < q_idx + h * lg if c * kb < (h + 1) * lg - 1 else None
              for c in range(ratio)] for h in range(qb // lg)]
    ones = jnp.ones((SUBLANES, lg), F32)
    state = super_block(qi, jnp.maximum(qi - 1, 0), logits(qi, 0), None, (ones,) * len(units), masks)

    def trip(it, state):
        ks = qi - 1 - it
        return super_block(ks, jnp.maximum(ks - 1, 0), state[0], state[1], state[2], None)

    _, w_last, _ = lax.fori_loop(0, qi, trip, state)
    accumulate(0, len(units) - 1, w_last)
    o = jnp.concatenate([acc_ref[0], acc_ref[1]], axis=0).T
    og_ref[...] = (o * sg_ref[...].astype(F32)).astype(og_ref.dtype)


def _sb_prompt_call(q, kp, vt, sg, n_seq):
    t_total, sb_w = sg.shape
    seq_len = t_total // n_seq
    qb = SB_QUERY_BLOCK
    assert seq_len % qb == 0
    n_q = seq_len // qb
    n_hp = sb_w // LANES
    return pl.pallas_call(
        _sb_prompt_body,
        grid=(n_seq, n_hp, n_q),
        in_specs=[
            pl.BlockSpec((qb, 2 * LANES), lambda b, hp, qi: (b * n_q + qi, hp)),
            pl.BlockSpec((seq_len, 2 * LANES), lambda b, hp, qi: (b, hp)),
            pl.BlockSpec((n_q, LANES, qb), lambda b, hp, qi: (b, hp, 0)),
            pl.BlockSpec((qb, LANES), lambda b, hp, qi: (b * n_q + qi, hp)),
        ],
        out_specs=pl.BlockSpec((qb, LANES), lambda b, hp, qi: (b * n_q + qi, hp)),
        scratch_shapes=[pltpu.VMEM((2, SB_DH, qb), F32)],
        out_shape=jax.ShapeDtypeStruct((t_total, sb_w), BF16),
        compiler_params=_params("parallel", "parallel", "arbitrary"),
        name="sb_attn_prompt",
    )(q, kp, vt, sg)


def _sb_sample_body(pt_ref, q_ref, kn_ref, vn_ref, sg_ref, bias_ref, *rest, n_pages, n_new):
    k_pages = rest[:n_pages]
    v_pages = rest[n_pages:2 * n_pages]
    og_ref = rest[2 * n_pages]
    kb_ref, vb_ref, z_ref, inc_ref, w_ref = rest[2 * n_pages + 1:]
    del pt_ref
    past = n_pages * PAGE_SIZE
    sb_w = SB_HEADS * SB_DH
    n_lane = SB_HEADS * n_new
    pad_rows = PAGE_SIZE - n_new

    for p in range(n_pages):
        kb_ref[p * PAGE_SIZE:(p + 1) * PAGE_SIZE, :] = k_pages[p][...].astype(BF16)
        vb_ref[p * PAGE_SIZE:(p + 1) * PAGE_SIZE, :] = v_pages[p][...].astype(BF16)
    zeros_pad = jnp.zeros((pad_rows, sb_w), F32)
    kb_ref[past:past + PAGE_SIZE, :] = jnp.concatenate([kn_ref[...], zeros_pad], axis=0).astype(BF16)
    vb_ref[past:past + PAGE_SIZE, :] = jnp.concatenate([vn_ref[...], zeros_pad], axis=0).astype(BF16)

    q = q_ref[...]
    q_rows = jnp.concatenate([q] * SB_HEADS, axis=0)
    row_head = lax.broadcasted_iota(jnp.int32, (n_lane, sb_w), 0) // n_new
    lane_head = lax.broadcasted_iota(jnp.int32, (n_lane, sb_w), 1) // SB_DH
    head_mask = row_head == lane_head
    q_bd = jnp.where(head_mask, q_rows, 0.0).T.astype(BF16)

    half = (n_pages // 2) * PAGE_SIZE
    z_ref[0:half, :] = _dot(kb_ref[0:half, :], q_bd) + bias_ref[...]
    z_ref[half:, :] = _dot(kb_ref[half:, :], q_bd) + bias_ref[...]
    w_ref[past + n_new:past + PAGE_SIZE, :] = jnp.zeros((pad_rows, n_lane), F32)

    key_idx = lax.broadcasted_iota(jnp.int32, (n_new, n_lane), 0)
    q_idx = lax.broadcasted_iota(jnp.int32, (n_new, n_lane), 1) % n_new
    carry = jnp.ones((SUBLANES, n_lane), F32)
    w_new, carry = _sb_block(z_ref[past:past + n_new, :], carry, key_idx < q_idx)
    w_ref[past:past + n_new, :] = w_new

    def local(it, _):
        rows = pl.ds(pl.multiple_of(it * SUBLANES, SUBLANES), SUBLANES)
        beta, keep = _sb_beta_keep(z_ref[rows, :], None)
        inc, after = _sublane_suffix_products(keep)
        z_ref[rows, :] = beta * after
        inc_ref[rows, :] = inc
        return 0

    lax.fori_loop(0, past // SUBLANES, local, 0, unroll=8)

    def scan(it, carry):
        rows = pl.ds(pl.multiple_of(past - SUBLANES * (it + 1), SUBLANES), SUBLANES)
        w_ref[rows, :] = z_ref[rows, :] * carry
        return carry * jnp.broadcast_to(inc_ref[rows, :][0:1, :], (SUBLANES, n_lane))

    lax.fori_loop(0, past // SUBLANES, scan, carry, unroll=8)

    w_t = w_ref[...].T.astype(BF16)
    out = jnp.concatenate([_dot(w_t, vb_ref[:, 0:sb_w // 2]), _dot(w_t, vb_ref[:, sb_w // 2:])],
                          axis=1)
    out = jnp.where(head_mask, out, 0.0)
    o = out[0:n_new, :]
    for hd in range(1, SB_HEADS):
        o = o + out[hd * n_new:(hd + 1) * n_new, :]
    og_ref[...] = o * sg_ref[...]


def _sb_sample_call(q, k_new, v_new, sg, bias_lanes, ck, cv, page_table, n_new):
    t_total, sb_w = q.shape
    n_seq, n_pages = page_table.shape
    assert n_new == SUBLANES and SB_HEADS * n_new == LANES
    n_lane = SB_HEADS * n_new
    rows_pad = (n_pages + 1) * PAGE_SIZE
    row = pl.BlockSpec((n_new, sb_w), lambda b, pt: (b, 0))

    def page_spec(p):
        return pl.BlockSpec((None, PAGE_SIZE, sb_w), lambda b, pt: (pt[b, p], 0, 0))

    grid_spec = pltpu.PrefetchScalarGridSpec(
        num_scalar_prefetch=1,
        grid=(n_seq,),
        in_specs=[row, row, row, row, pl.BlockSpec((1, n_lane), lambda b, pt: (0, 0))]
        + [page_spec(p) for p in range(n_pages)] + [page_spec(p) for p in range(n_pages)],
        out_specs=row,
        scratch_shapes=[
            pltpu.VMEM((rows_pad, sb_w), BF16),
            pltpu.VMEM((rows_pad, sb_w), BF16),
            pltpu.VMEM((rows_pad, n_lane), F32),
            pltpu.VMEM((n_pages * PAGE_SIZE, n_lane), F32),
            pltpu.VMEM((rows_pad, n_lane), F32),
        ],
    )
    return pl.pallas_call(
        functools.partial(_sb_sample_body, n_pages=n_pages, n_new=n_new),
        grid_spec=grid_spec,
        out_shape=jax.ShapeDtypeStruct((t_total, sb_w), F32),
        compiler_params=_params("arbitrary"),
        name="sb_attn_sample",
    )(page_table, q, k_new, v_new, sg, bias_lanes, *([ck] * n_pages), *([cv] * n_pages))


def _rope_tables(pos):
    half = RET_DK // 2
    inv_freq = ROPE_BASE ** (-jnp.arange(half, dtype=F32) / half)
    ang = pos.astype(F32)[:, None] * inv_freq[None, :]
    cos, sin = jnp.cos(ang), jnp.sin(ang)
    return jnp.concatenate([cos, cos], axis=1), jnp.concatenate([-sin, sin], axis=1)


def _sb_perm_matrix(kb):
    row = jnp.arange(kb)
    n = kb // SUBLANES
    src = (row % SUBLANES) * n + (n - 1 - row // SUBLANES)
    return (src[:, None] == jnp.arange(kb)[None, :]).astype(BF16)


def _head_slots(vals):
    slots = jnp.zeros((SB_HEADS, LANES), F32).at[:, SB_DH:SB_DH + 2].set(vals)
    return slots.reshape(1, SB_HEADS * LANES)


def _bias_slots(half_bias):
    hi = half_bias.astype(BF16).astype(F32)
    lo = (half_bias - hi).astype(BF16).astype(F32)
    return _head_slots(jnp.stack([hi, lo], axis=1))


def _trunk(x, mods, blocks_per_seq, n_seq, rope, s0, attn_fn, w, act_dtype, permuted):
    (norm_g, ret_w_in, ret_norm_g, ret_w_out, kv_norm_g, w_kv, sb_q_g, sb_k_g,
     sb_w_in, sb_w_out, ones_bd, bias_slots) = w
    n_a = ret_w_in.shape[0]
    n_b = sb_w_in.shape[0]
    seq_len = x.shape[0] // n_seq
    chunk = math.gcd(seq_len, RET_CHUNK)
    tabs = _ret_tables(chunk, max(chunk, LANES))
    states = None
    for l in range(n_a):
        q, k, v, sg = _ret_proj_call(x, mods[l], norm_g[l], ret_w_in[l], rope[0], rope[1],
                                     blocks_per_seq, act_dtype)
        og, states = _ret_core_call(q, k, v, sg, tabs, ret_norm_g[l], s0, l, n_a, states,
                                    n_seq, act_dtype)
        x = _out_proj_call(og, ret_w_out[l], x, mods[l], blocks_per_seq)
    k_gain_row = jnp.tile(sb_k_g.astype(F32), SB_HEADS).reshape(1, -1)
    perm = _sb_perm_matrix(SB_KEY_BLOCK) if permuted else None
    kv = _kv_call(x, kv_norm_g, w_kv, k_gain_row, ones_bd, perm,
                  _head_slots(jnp.ones((SB_HEADS, 2), F32)) if permuted else None, n_seq)
    for j in range(n_b):
        l = n_a + j
        q_gain_row = jnp.tile(sb_q_g[j].astype(F32) * (0.5 * SB_DH ** -0.5), SB_HEADS).reshape(1, -1)
        q, sg = _sb_proj_call(x, mods[l], norm_g[l], sb_w_in[j], q_gain_row, ones_bd,
                              bias_slots[j] if permuted else None, blocks_per_seq, act_dtype)
        og = attn_fn(j, q, sg, kv)
        x = _out_proj_call(og, sb_w_out[j], x, mods[l], blocks_per_seq)
    return x, states, kv[0], kv[1]


def kernel(x_prompt, x_sample, state_ret, cache_k, cache_v, page_table, c_prompt, c_sample,
           ada_w, ada_b, norm_g, ret_w_in, ret_norm_g, ret_w_out,
           kv_norm_g, w_kv, sb_q_g, sb_k_g, sb_w_in, sb_w_out, sb_bias):
    n_p, len_p, d = x_prompt.shape
    n_s, len_s, _ = x_sample.shape
    depth = ada_w.shape[0]
    sb_w = SB_HEADS * SB_DH
    past_len = page_table.shape[1] * PAGE_SIZE

    n_c = n_p + n_s
    c_rows = -(-n_c // 16) * 16
    c_all = jnp.concatenate([c_prompt.astype(F32), c_sample.astype(F32),
                             jnp.zeros((c_rows - n_c, d), F32)], axis=0)
    mod = _ada_call(c_all, ada_w.astype(F32), ada_b.astype(F32))
    mods_p = [mod[l, :n_p].reshape(n_p, 1, 3 * d) for l in range(depth)]
    mods_s = [jnp.repeat(mod[l, n_p:n_c], len_s, axis=0).reshape(1, n_s * len_s, 3 * d)
              for l in range(depth)]

    ones_bd = (jnp.arange(sb_w)[:, None] // SB_DH == jnp.arange(sb_w)[None, :] // SB_DH).astype(BF16)
    half_bias = 0.5 * sb_bias.astype(F32)
    weights = (norm_g.astype(F32), ret_w_in.astype(BF16), ret_norm_g.astype(F32), ret_w_out.astype(BF16),
               kv_norm_g.astype(F32), w_kv.astype(BF16), sb_q_g, sb_k_g,
               sb_w_in.astype(BF16), sb_w_out.astype(BF16), ones_bd,
               [_bias_slots(half_bias[j]) for j in range(sb_bias.shape[0])])

    rope_p = _rope_tables(jnp.arange(len_p, dtype=jnp.int32))
    tm_p = min(TOKEN_BLOCK, len_p)

    def attn_p(j, q, sg, kv):
        return _sb_prompt_call(q, kv[2], kv[3], sg, n_p)

    y_p, st_p, k_p, v_p = _trunk(x_prompt.astype(F32).reshape(n_p * len_p, d), mods_p, len_p // tm_p,
                                 n_p, rope_p, None, attn_p, weights, BF16, True)

    pos_s = past_len + jnp.arange(len_s, dtype=jnp.int32)
    rope_s = tuple(jnp.tile(t, (n_s, 1)) for t in _rope_tables(pos_s))

    n_pool = cache_k.shape[0]
    ck = cache_k.astype(F32).reshape(n_pool, PAGE_SIZE, sb_w)
    cv = cache_v.astype(F32).reshape(n_pool, PAGE_SIZE, sb_w)

    def attn_s(j, q, sg, kv):
        bias_lanes = jnp.repeat(half_bias[j], len_s).reshape(1, SB_HEADS * len_s)
        return _sb_sample_call(q, kv[0], kv[1], sg, bias_lanes, ck, cv, page_table, len_s)

    y_s, st_s, k_s, v_s = _trunk(x_sample.astype(F32).reshape(n_s * len_s, d), mods_s, None,
                                 n_s, rope_s, state_ret.astype(F32), attn_s, weights, F32, False)

    return (y_p.reshape(n_p, len_p, d).astype(x_prompt.dtype),
            y_s.reshape(n_s, len_s, d).astype(x_sample.dtype),
            st_p, st_s,
            k_p.reshape(n_p, len_p, SB_HEADS, SB_DH), v_p.reshape(n_p, len_p, SB_HEADS, SB_DH),
            k_s.reshape(n_s, len_s, SB_HEADS, SB_DH), v_s.reshape(n_s, len_s, SB_HEADS, SB_DH))
```

```python
import functools
import math

import jax
import jax.numpy as jnp
from jax import lax
from jax.experimental import pallas as pl
from jax.experimental.pallas import tpu as pltpu

F32 = jnp.float32
BF16 = jnp.bfloat16

RET_HEADS = 8
RET_DK = 128
RET_DV = 256
RET_CHUNK = 128
SB_HEADS = 16
SB_DH = 64
PAGE_SIZE = 128
ROPE_BASE = 10000.0
EPS = 1e-6

LANES = 128
SUBLANES = 8
VMEM_LIMIT_BYTES = 56 * 1024 * 1024

TOKEN_BLOCK = 512
SB_KEY_BLOCK = 256
SB_QUERY_BLOCK = 512
SB_LANE_GROUP = 256
ADA_COL_BLOCK = 512


def _params(*sem):
    return pltpu.CompilerParams(dimension_semantics=sem, vmem_limit_bytes=VMEM_LIMIT_BYTES)


def _resident(a, n_grid):
    zeros = (0,) * a.ndim
    index_map = {1: lambda t: zeros, 2: lambda b, t: zeros}[n_grid]
    return pl.BlockSpec(a.shape, index_map, pipeline_mode=pl.Buffered(1))


def _split_bf16(a):
    hi = a.astype(BF16)
    lo = (a - hi.astype(F32)).astype(BF16)
    return hi, lo


def _dot(a, b):
    return jnp.dot(a, b, preferred_element_type=F32)


def _dot3(a, b):
    a_hi, a_lo = _split_bf16(a)
    b_hi, b_lo = _split_bf16(b)
    return _dot(a_hi, b_hi) + _dot(a_lo, b_hi) + _dot(a_hi, b_lo)


def _silu(x):
    return x / (1.0 + jnp.exp(-x))


def _ada_body(c_ref, w_ref, b_ref, o_ref):
    o_ref[...] = _dot3(_silu(c_ref[...]), w_ref[...]) + b_ref[...]


def _ada_call(c_pad, ada_w, ada_b):
    depth, d, d3 = ada_w.shape
    rows = c_pad.shape[0]
    tn = ADA_COL_BLOCK
    return pl.pallas_call(
        _ada_body,
        grid=(depth, d3 // tn),
        in_specs=[
            pl.BlockSpec((rows, d), lambda l, j: (0, 0)),
            pl.BlockSpec((None, d, tn), lambda l, j: (l, 0, j)),
            pl.BlockSpec((None, 1, tn), lambda l, j: (l, 0, j)),
        ],
        out_specs=pl.BlockSpec((None, rows, tn), lambda l, j: (l, 0, j)),
        out_shape=jax.ShapeDtypeStruct((depth, rows, d3), F32),
        compiler_params=_params("parallel", "parallel"),
        name="ada_mod",
    )(c_pad, ada_w, ada_b.reshape(depth, 1, d3))


def _modulated(x, g, m):
    d = x.shape[-1]
    r = lax.rsqrt(jnp.mean(x * x, axis=-1, keepdims=True) + EPS)
    return x * r * g * (1.0 + m[:, d:2 * d]) + m[:, :d]


def _mod_spec(mod, tm, blocks_per_seq):
    d3 = mod.shape[-1]
    if mod.shape[1] == 1:
        return pl.BlockSpec((None, 1, d3), lambda t: (t // blocks_per_seq, 0, 0))
    return pl.BlockSpec((None, tm, d3), lambda t: (0, t, 0))


def _tab_spec(tab, tm, blocks_per_seq):
    if blocks_per_seq is None:
        return pl.BlockSpec((tm, LANES), lambda t: (t, 0))
    return pl.BlockSpec((tm, LANES), lambda t: (t % blocks_per_seq, 0))


def _head_sum64(x_sq, ones_bd):
    return _dot(x_sq.astype(BF16), ones_bd)


def _ret_proj_body(x_ref, m_ref, g_ref, w_ref, cos_ref, sin_ref, q_ref, k_ref, v_ref, sg_ref):
    h = _modulated(x_ref[...], g_ref[...], m_ref[...]).astype(BF16)
    cosf = cos_ref[...]
    sinf = sin_ref[...]
    qk_w = RET_HEADS * RET_DK
    v_w = RET_HEADS * RET_DV

    def rotary(p, scale):
        outs = []
        for hd in range(RET_HEADS):
            sl = p[:, hd * RET_DK:(hd + 1) * RET_DK]
            rot = pltpu.roll(sl, RET_DK // 2, axis=1)
            o = sl * cosf + rot * sinf
            outs.append(o if scale is None else o * scale)
        return jnp.concatenate(outs, axis=1)

    q = _dot(h, w_ref[:, 0:qk_w])
    q_ref[...] = rotary(q, None).astype(q_ref.dtype)
    k = _dot(h, w_ref[:, qk_w:2 * qk_w])
    k_ref[...] = rotary(k, RET_DK ** -0.5).astype(k_ref.dtype)
    v_ref[...] = _dot(h, w_ref[:, 2 * qk_w:2 * qk_w + v_w]).astype(v_ref.dtype)
    g = _dot(h, w_ref[:, 2 * qk_w + v_w:2 * qk_w + 2 * v_w])
    sg_ref[...] = _silu(g).astype(sg_ref.dtype)


def _ret_proj_call(x, mod, g, w_bf16, cosf, sinf, blocks_per_seq, out_dtype):
    t_total, d = x.shape
    tm = min(TOKEN_BLOCK, t_total)
    qk_w = RET_HEADS * RET_DK
    v_w = RET_HEADS * RET_DV
    row = lambda w: pl.BlockSpec((tm, w), lambda t: (t, 0))
    return pl.pallas_call(
        _ret_proj_body,
        grid=(t_total // tm,),
        in_specs=[
            row(d),
            _mod_spec(mod, tm, blocks_per_seq),
            pl.BlockSpec((1, d), lambda t: (0, 0)),
            _resident(w_bf16, 1),
            _tab_spec(cosf, tm, blocks_per_seq if mod.shape[1] == 1 else None),
            _tab_spec(sinf, tm, blocks_per_seq if mod.shape[1] == 1 else None),
        ],
        out_specs=[row(qk_w), row(qk_w), row(v_w), row(v_w)],
        out_shape=[
            jax.ShapeDtypeStruct((t_total, qk_w), out_dtype),
            jax.ShapeDtypeStruct((t_total, qk_w), out_dtype),
            jax.ShapeDtypeStruct((t_total, v_w), out_dtype),
            jax.ShapeDtypeStruct((t_total, v_w), out_dtype),
        ],
        compiler_params=_params("parallel"),
        name="ret_proj",
    )(x, mod, g.reshape(1, d), w_bf16, cosf, sinf)


def _ret_core_body(*refs, chunk, chunk_pad, n_chunks, has_s0, has_stack, layer):
    q_ref, k_ref, v_ref, sg_ref, dm_ref, qd_ref, kd_ref, cd_ref, ng_ref = refs[:9]
    s0_ref = refs[9] if has_s0 else None
    og_ref, stack_ref = refs[9 + has_s0 + has_stack:]
    mxu = BF16 if chunk >= 16 else F32
    s_ref = stack_ref if has_stack else stack_ref.at[layer]

    @pl.when(pl.program_id(1) == 0)
    def _():
        if not has_stack:
            stack_ref[...] = jnp.zeros_like(stack_ref)
        if has_s0:
            s_ref[...] = s0_ref[...]
        elif has_stack:
            s_ref[...] = jnp.zeros_like(s_ref)

    def pad_rows(a):
        if chunk_pad == chunk:
            return a
        return jnp.concatenate([a, jnp.zeros((chunk_pad - chunk, a.shape[1]), a.dtype)], axis=0)

    def one_chunk(ci, carry):
        r0 = pl.multiple_of(ci * chunk, chunk)
        rows = pl.ds(r0, chunk)
        for hd in range(RET_HEADS):
            qc = q_ref[rows, hd * RET_DK:(hd + 1) * RET_DK]
            kc = pad_rows(k_ref[rows, hd * RET_DK:(hd + 1) * RET_DK])
            vc = pad_rows(v_ref[rows, hd * RET_DV:(hd + 1) * RET_DV]).astype(mxu)
            s_old = s_ref[hd]
            scores = lax.dot_general(qc.astype(mxu), kc.astype(mxu), (((1,), (1,)), ((), ())),
                                     preferred_element_type=F32) * dm_ref[hd]
            intra = _dot(scores.astype(mxu), vc)
            qdec = (qc.astype(F32) * qd_ref[hd]).astype(mxu)
            cross = _dot(qdec, s_old.astype(mxu))
            o = intra + cross
            kdec = (kc.astype(F32) * kd_ref[hd]).astype(mxu)
            s_ref[hd] = s_old * cd_ref[hd] + lax.dot_general(
                kdec, vc, (((0,), (0,)), ((), ())), preferred_element_type=F32)
            r = lax.rsqrt(jnp.mean(o * o, axis=-1, keepdims=True) + EPS)
            cols = slice(hd * RET_DV, (hd + 1) * RET_DV)
            og = o * r * ng_ref[:, cols] * sg_ref[rows, cols].astype(F32)
            og_ref[rows, cols] = og.astype(og_ref.dtype)
        return carry

    if n_chunks == 1:
        one_chunk(0, 0)
    else:
        lax.fori_loop(0, n_chunks, one_chunk, 0)


def _ret_core_call(q, k, v, sg, tabs, norm_g, s0, layer, n_layers, states, n_seq, out_dtype):
    t_total = q.shape[0]
    seq_len = t_total // n_seq
    chunk = math.gcd(seq_len, RET_CHUNK)
    chunk_pad = max(chunk, LANES)
    tm = min(TOKEN_BLOCK, seq_len)
    n_chunks = tm // chunk
    nblk = seq_len // tm
    dm, qd, kd, cd = tabs
    qk_w = RET_HEADS * RET_DK
    v_w = RET_HEADS * RET_DV
    row = lambda w: pl.BlockSpec((tm, w), lambda b, t: (b * nblk + t, 0))
    full = lambda a: _resident(a, 2)
    state_spec = pl.BlockSpec((None, None, RET_HEADS, RET_DK, RET_DV),
                              lambda b, t: (layer, b, 0, 0, 0))
    in_specs = [row(qk_w), row(qk_w), row(v_w), row(v_w), full(dm), full(qd), full(kd), full(cd),
                pl.BlockSpec((1, v_w), lambda b, t: (0, 0))]
    args = [q, k, v, sg, dm, qd, kd, cd, norm_g.reshape(1, v_w)]
    if s0 is not None:
        in_specs.append(state_spec)
        args.append(s0)
    aliases = {}
    if states is not None:
        aliases = {len(args): 1}
        in_specs.append(pl.BlockSpec(memory_space=pl.ANY))
        args.append(states)
    body = functools.partial(_ret_core_body, chunk=chunk, chunk_pad=chunk_pad, n_chunks=n_chunks,
                             has_s0=s0 is not None, has_stack=states is not None, layer=layer)
    out_state_spec = state_spec if states is not None else pl.BlockSpec(
        (n_layers, None, RET_HEADS, RET_DK, RET_DV), lambda b, t: (0, b, 0, 0, 0))
    return pl.pallas_call(
        body,
        grid=(n_seq, nblk),
        in_specs=in_specs,
        out_specs=[row(v_w), out_state_spec],
        out_shape=[
            jax.ShapeDtypeStruct((t_total, v_w), out_dtype),
            jax.ShapeDtypeStruct((n_layers, n_seq, RET_HEADS, RET_DK, RET_DV), F32),
        ],
        input_output_aliases=aliases,
        compiler_params=_params("parallel", "arbitrary"),
        name="ret_core",
    )(*args)


def _ret_tables(chunk, chunk_pad):
    hds = jnp.arange(RET_HEADS, dtype=F32)
    log_gamma = jnp.log1p(-jnp.exp2(-5.0 - hds))
    idx = jnp.arange(chunk, dtype=F32)
    diff = idx[:, None] - idx[None, :]
    dmask = jnp.where(diff[None] >= 0,
                      jnp.exp(log_gamma[:, None, None] * jnp.maximum(diff, 0.0)[None]), 0.0)
    dmask = jnp.pad(dmask, ((0, 0), (0, 0), (0, chunk_pad - chunk)))
    q_dec = jnp.exp(log_gamma[:, None] * (idx[None, :] + 1.0))
    k_dec = jnp.exp(log_gamma[:, None] * (chunk - 1.0 - idx[None, :]))
    k_dec = jnp.pad(k_dec, ((0, 0), (0, chunk_pad - chunk)))
    c_dec = jnp.exp(log_gamma * chunk)
    qd = jnp.broadcast_to(q_dec[:, :, None], (RET_HEADS, chunk, RET_DK))
    kd = jnp.broadcast_to(k_dec[:, :, None], (RET_HEADS, chunk_pad, RET_DK))
    cd = jnp.broadcast_to(c_dec[:, None, None], (RET_HEADS, 1, RET_DV))
    return dmask.astype(F32), qd.astype(F32), kd.astype(F32), cd.astype(F32)


def _out_proj_body(a_ref, w_ref, x_ref, m_ref, o_ref):
    d = x_ref.shape[-1]
    y = _dot(a_ref[...].astype(BF16), w_ref[...])
    o_ref[...] = x_ref[...] + m_ref[:, 2 * d:3 * d] * y


def _out_proj_call(a, w_bf16, x, mod, blocks_per_seq):
    t_total, d = x.shape
    tm = min(TOKEN_BLOCK, t_total)
    ka = a.shape[1]
    return pl.pallas_call(
        _out_proj_body,
        grid=(t_total // tm,),
        in_specs=[
            pl.BlockSpec((tm, ka), lambda t: (t, 0)),
            _resident(w_bf16, 1),
            pl.BlockSpec((tm, d), lambda t: (t, 0)),
            _mod_spec(mod, tm, blocks_per_seq),
        ],
        out_specs=pl.BlockSpec((tm, d), lambda t: (t, 0)),
        out_shape=jax.ShapeDtypeStruct((t_total, d), F32),
        compiler_params=_params("parallel"),
        name="out_proj",
    )(a, w_bf16, x, mod)


def _kv_body(*refs, permuted):
    if permuted:
        (x_ref, g_ref, w_ref, kg_ref, ones_ref, perm_ref, slots_ref,
         k_ref, v_ref, kp_ref, vt_ref) = refs
    else:
        x_ref, g_ref, w_ref, kg_ref, ones_ref, k_ref, v_ref = refs
    x = x_ref[...]
    sb_w = SB_HEADS * SB_DH
    r = lax.rsqrt(jnp.mean(x * x, axis=-1, keepdims=True) + EPS)
    xn = (x * r * g_ref[...]).astype(BF16)
    k = _dot(xn, w_ref[:, :sb_w])
    v = _dot(xn, w_ref[:, sb_w:])
    ms = _head_sum64(k * k, ones_ref[...]) * (1.0 / SB_DH)
    k = k * lax.rsqrt(ms + EPS) * kg_ref[...]
    k_ref[...] = k
    v_ref[...] = v
    if permuted:
        kb = SB_KEY_BLOCK
        perm = perm_ref[...]
        cols = []
        for j in range(x.shape[0] // kb):
            rows = slice(j * kb, (j + 1) * kb)
            kp_ref[rows, :] = _expand_heads(_dot(perm, k[rows].astype(BF16)),
                                            slots_ref[...]).astype(BF16)
            cols.append(_dot(perm, v[rows].astype(BF16)).T.astype(BF16))
        vt_ref[0] = jnp.concatenate(cols, axis=1)


def _kv_call(x, g, w_bf16, k_gain_row, ones_bd, perm, one_slots, n_seq):
    t_total, d = x.shape
    tm = min(TOKEN_BLOCK, t_total)
    sb_w = SB_HEADS * SB_DH
    row = lambda w: pl.BlockSpec((tm, w), lambda t: (t, 0))
    const = lambda a: _resident(a, 1)
    in_specs = [row(d), pl.BlockSpec((1, d), lambda t: (0, 0)), const(w_bf16),
                const(k_gain_row), const(ones_bd)]
    args = [x, g.reshape(1, d), w_bf16, k_gain_row, ones_bd]
    out_specs = [row(sb_w), row(sb_w)]
    out_shape = [jax.ShapeDtypeStruct((t_total, sb_w), F32), jax.ShapeDtypeStruct((t_total, sb_w), F32)]
    if perm is not None:
        sup = SB_QUERY_BLOCK
        assert sup % tm == 0 and tm % SB_KEY_BLOCK == 0
        per_sup = sup // tm
        kp_w = SB_HEADS * LANES
        in_specs += [const(perm), const(one_slots)]
        args += [perm, one_slots]
        out_specs += [row(kp_w),
                      pl.BlockSpec((1, sb_w, tm), lambda t: (t // per_sup, 0, t % per_sup))]
        out_shape += [jax.ShapeDtypeStruct((t_total, kp_w), BF16),
                      jax.ShapeDtypeStruct((t_total // sup, sb_w, sup), BF16)]
    return pl.pallas_call(
        functools.partial(_kv_body, permuted=perm is not None),
        grid=(t_total // tm,),
        in_specs=in_specs,
        out_specs=out_specs,
        out_shape=out_shape,
        compiler_params=_params("parallel"),
        name="shared_kv",
    )(*args)


def _expand_heads(x, slots):
    lane = lax.broadcasted_iota(jnp.int32, (x.shape[0], LANES), 1)
    cols = []
    for hd in range(SB_HEADS):
        col = x[:, (hd // 2) * LANES:(hd // 2 + 1) * LANES]
        if hd % 2:
            col = pltpu.roll(col, SB_DH, axis=1)
        cols.append(jnp.where(lane < SB_DH, col, slots[:, hd * LANES:(hd + 1) * LANES]))
    return jnp.concatenate(cols, axis=1)


def _sb_proj_body(*refs, expand):
    if expand:
        x_ref, m_ref, g_ref, w_ref, qg_ref, ones_ref, slots_ref, q_ref, sg_ref = refs
    else:
        x_ref, m_ref, g_ref, w_ref, qg_ref, ones_ref, q_ref, sg_ref = refs
    h = _modulated(x_ref[...], g_ref[...], m_ref[...]).astype(BF16)
    sb_w = SB_HEADS * SB_DH
    q = _dot(h, w_ref[:, :sb_w])
    ms = _head_sum64(q * q, ones_ref[...]) * (1.0 / SB_DH)
    q = q * lax.rsqrt(ms + EPS) * qg_ref[...]
    if expand:
        q = _expand_heads(q, slots_ref[...])
    q_ref[...] = q.astype(q_ref.dtype)
    sg_ref[...] = _silu(_dot(h, w_ref[:, sb_w:])).astype(sg_ref.dtype)


def _sb_proj_call(x, mod, g, w_bf16, q_gain_row, ones_bd, bias_slots, blocks_per_seq, out_dtype):
    t_total, d = x.shape
    tm = min(TOKEN_BLOCK, t_total)
    sb_w = SB_HEADS * SB_DH
    row = lambda w: pl.BlockSpec((tm, w), lambda t: (t, 0))
    const = lambda a: _resident(a, 1)
    expand = bias_slots is not None
    q_w = SB_HEADS * LANES if expand else sb_w
    in_specs = [row(d), _mod_spec(mod, tm, blocks_per_seq), pl.BlockSpec((1, d), lambda t: (0, 0)),
                const(w_bf16), const(q_gain_row), const(ones_bd)]
    args = [x, mod, g.reshape(1, d), w_bf16, q_gain_row, ones_bd]
    if expand:
        in_specs.append(const(bias_slots))
        args.append(bias_slots)
    return pl.pallas_call(
        functools.partial(_sb_proj_body, expand=expand),
        grid=(t_total // tm,),
        in_specs=in_specs,
        out_specs=[row(q_w), row(sb_w)],
        out_shape=[jax.ShapeDtypeStruct((t_total, q_w), out_dtype),
                   jax.ShapeDtypeStruct((t_total, sb_w), out_dtype)],
        compiler_params=_params("parallel"),
        name="sb_proj",
    )(*args)


def _sb_beta_keep(u, mask):
    half_t = 0.5 * jnp.tanh(u)
    keep, beta = 0.5 - half_t, 0.5 + half_t
    if mask is not None:
        keep, beta = jnp.where(mask, keep, 1.0), jnp.where(mask, beta, 0.0)
    return beta, keep


def _sublane_suffix_products(x):
    sub = lax.broadcasted_iota(jnp.int32, x.shape, 0)
    inc = x
    for step in (1, 2, 4):
        shifted = pltpu.roll(inc, SUBLANES - step, axis=0)
        inc = inc * jnp.where(sub < SUBLANES - step, shifted, 1.0)
    after = jnp.where(sub < SUBLANES - 1, pltpu.roll(inc, SUBLANES - 1, axis=0), 1.0)
    return inc, after


def _sb_blocks(us, carry, masks):
    kb, nq = us[0].shape
    n = kb // SUBLANES
    runs = [None] * len(us)
    local = [[None] * n for _ in us]
    for i in range(n):
        rows = slice(i * SUBLANES, (i + 1) * SUBLANES)
        for b, (u, mask) in enumerate(zip(us, masks)):
            h_i = 0.5 * jnp.tanh(u[rows, :])
            if mask is not None:
                h_i = jnp.where(mask[rows, :], h_i, -0.5)
            beta_i, keep_i = 0.5 + h_i, 0.5 - h_i
            local[b][i] = beta_i if runs[b] is None else beta_i * runs[b]
            runs[b] = keep_i if runs[b] is None else runs[b] * keep_i
    ws = []
    for b in range(len(us)):
        inc, after = _sublane_suffix_products(runs[b])
        scale = carry * after
        carry = carry * jnp.broadcast_to(inc[0:1, :], (SUBLANES, nq))
        ws.append(jnp.concatenate([local[b][i] * scale for i in range(n)], axis=0))
    return ws, carry


def _sb_block(z_t, carry, mask):
    ws, carry = _sb_blocks([z_t], carry, [mask])
    return ws[0], carry


def _sb_key_offsets(kb, nq):
    row = lax.broadcasted_iota(jnp.int32, (kb, nq), 0)
    n = kb // SUBLANES
    return (row & (SUBLANES - 1)) * n + (n - 1 - (row >> 3))


def _sb_prompt_body(q_ref, kp_ref, vt_ref, sg_ref, og_ref, acc_ref):
    qi = pl.program_id(2)
    qb = q_ref.shape[0]
    kb = SB_KEY_BLOCK
    ratio = qb // kb
    lg = SB_LANE_GROUP
    units = [(j, h) for j in range(2) for h in range(qb // lg)]
    q_unit = [q_ref[h * lg:(h + 1) * lg, j * LANES:(j + 1) * LANES] for j, h in units]
    order = list(range(ratio - 1, -1, -1))

    def logits(ks, u):
        r0 = pl.multiple_of(ks * qb, qb)
        j = units[u][0]
        return lax.dot_general(kp_ref[pl.ds(r0, qb), j * LANES:(j + 1) * LANES], q_unit[u],
                               (((1,), (1,)), ((), ())), preferred_element_type=F32)

    def weights(z_t, carry, masks):
        live = [c for c in order if masks is None or masks[c] is not None]
        ws, carry = _sb_blocks([z_t[c * kb:(c + 1) * kb, :] for c in live], carry,
                               [None if masks is None else masks[c] for c in live])
        by_block = dict(zip(live, ws))
        rows = [by_block[c] if c in by_block else jnp.zeros((kb, lg), F32) for c in range(ratio)]
        return jnp.concatenate(rows, axis=0).astype(BF16), carry

    def accumulate(ks, u, w):
        j, h = units[u]
        acc_ref[j, :, h * lg:(h + 1) * lg] += _dot(vt_ref[ks, j * SB_DH:(j + 1) * SB_DH, :], w)

    def super_block(ks, ks_next, z_first, w_last, carries, masks):
        z_t, w_prev = z_first, w_last
        out = []
        for u in range(len(units)):
            z_ahead = logits(ks, u + 1) if u + 1 < len(units) else logits(ks_next, 0)
            if u > 0:
                accumulate(ks, u - 1, w_prev)
            elif w_prev is not None:
                accumulate(ks + 1, len(units) - 1, w_prev)
            w_prev, carry = weights(z_t, carries[u], None if masks is None else masks[units[u][1]])
            out.append(carry)
            z_t = z_ahead
        return z_t, w_prev, tuple(out)

    acc_ref[...] = jnp.zeros_like(acc_ref)
    key_off = _sb_key_offsets(kb, lg)
    q_idx = lax.broadcasted_iota(jnp.int32, (kb, lg), 1)
    masks = [[key_off + c * kb < q_idx + h * lg if c * kb < (h + 1) * lg - 1 else None
              for c in range(ratio)] for h in range(qb // lg)]
    ones = jnp.ones((SUBLANES, lg), F32)
    state = super_block(qi, jnp.maximum(qi - 1, 0), logits(qi, 0), None, (ones,) * len(units), masks)

    def trips(first, count):
        def body(it, state):
            ks = first - count * it
            for c in range(count):
                state = super_block(ks - c, jnp.maximum(ks - c - 1, 0), *state, None)
            return state
        return body

    odd = qi & 1
    state = lax.fori_loop(0, odd, trips(qi - 1, 1), state)
    _, w_last, _ = lax.fori_loop(0, qi >> 1, trips(qi - 1 - odd, 2), state)
    accumulate(0, len(units) - 1, w_last)
    o = jnp.concatenate([acc_ref[0], acc_ref[1]], axis=0).T
    og_ref[...] = (o * sg_ref[...].astype(F32)).astype(og_ref.dtype)


def _sb_prompt_call(q, kp, vt, sg, n_seq):
    t_total, sb_w = sg.shape
    seq_len = t_total // n_seq
    qb = SB_QUERY_BLOCK
    assert seq_len % qb == 0
    n_q = seq_len // qb
    n_hp = sb_w // LANES
    return pl.pallas_call(
        _sb_prompt_body,
        grid=(n_seq, n_hp, n_q),
        in_specs=[
            pl.BlockSpec((qb, 2 * LANES), lambda b, hp, qi: (b * n_q + qi, hp)),
            pl.BlockSpec((seq_len, 2 * LANES), lambda b, hp, qi: (b, hp)),
            pl.BlockSpec((n_q, LANES, qb), lambda b, hp, qi: (b, hp, 0)),
            pl.BlockSpec((qb, LANES), lambda b, hp, qi: (b * n_q + qi, hp)),
        ],
        out_specs=pl.BlockSpec((qb, LANES), lambda b, hp, qi: (b * n_q + qi, hp)),
        scratch_shapes=[pltpu.VMEM((2, SB_DH, qb), F32)],
        out_shape=jax.ShapeDtypeStruct((t_total, sb_w), BF16),
        compiler_params=_params("parallel", "parallel", "arbitrary"),
        name="sb_attn_prompt",
    )(q, kp, vt, sg)


def _sb_sample_body(pt_ref, q_ref, kn_ref, vn_ref, sg_ref, bias_ref, *rest, n_pages, n_new):
    k_pages = rest[:n_pages]
    v_pages = rest[n_pages:2 * n_pages]
    og_ref = rest[2 * n_pages]
    kb_ref, vb_ref, z_ref, inc_ref, w_ref = rest[2 * n_pages + 1:]
    del pt_ref
    past = n_pages * PAGE_SIZE
    sb_w = SB_HEADS * SB_DH
    n_lane = SB_HEADS * n_new
    pad_rows = PAGE_SIZE - n_new

    for p in range(n_pages):
        kb_ref[p * PAGE_SIZE:(p + 1) * PAGE_SIZE, :] = k_pages[p][...].astype(BF16)
        vb_ref[p * PAGE_SIZE:(p + 1) * PAGE_SIZE, :] = v_pages[p][...].astype(BF16)
    zeros_pad = jnp.zeros((pad_rows, sb_w), F32)
    kb_ref[past:past + PAGE_SIZE, :] = jnp.concatenate([kn_ref[...], zeros_pad], axis=0).astype(BF16)
    vb_ref[past:past + PAGE_SIZE, :] = jnp.concatenate([vn_ref[...], zeros_pad], axis=0).astype(BF16)

    q = q_ref[...]
    q_rows = jnp.concatenate([q] * SB_HEADS, axis=0)
    row_head = lax.broadcasted_iota(jnp.int32, (n_lane, sb_w), 0) // n_new
    lane_head = lax.broadcasted_iota(jnp.int32, (n_lane, sb_w), 1) // SB_DH
    head_mask = row_head == lane_head
    q_bd = jnp.where(head_mask, q_rows, 0.0).T.astype(BF16)

    half = (n_pages // 2) * PAGE_SIZE
    z_ref[0:half, :] = _dot(kb_ref[0:half, :], q_bd) + bias_ref[...]
    z_ref[half:, :] = _dot(kb_ref[half:, :], q_bd) + bias_ref[...]
    w_ref[past + n_new:past + PAGE_SIZE, :] = jnp.zeros((pad_rows, n_lane), F32)

    key_idx = lax.broadcasted_iota(jnp.int32, (n_new, n_lane), 0)
    q_idx = lax.broadcasted_iota(jnp.int32, (n_new, n_lane), 1) % n_new
    carry = jnp.ones((SUBLANES, n_lane), F32)
    w_new, carry = _sb_block(z_ref[past:past + n_new, :], carry, key_idx < q_idx)
    w_ref[past:past + n_new, :] = w_new

    def local(it, _):
        rows = pl.ds(pl.multiple_of(it * SUBLANES, SUBLANES), SUBLANES)
        beta, keep = _sb_beta_keep(z_ref[rows, :], None)
        inc, after = _sublane_suffix_products(keep)
        z_ref[rows, :] = beta * after
        inc_ref[rows, :] = inc
        return 0

    lax.fori_loop(0, past // SUBLANES, local, 0, unroll=8)

    def scan(it, carry):
        rows = pl.ds(pl.multiple_of(past - SUBLANES * (it + 1), SUBLANES), SUBLANES)
        w_ref[rows, :] = z_ref[rows, :] * carry
        return carry * jnp.broadcast_to(inc_ref[rows, :][0:1, :], (SUBLANES, n_lane))

    lax.fori_loop(0, past // SUBLANES, scan, carry, unroll=8)

    w_t = w_ref[...].T.astype(BF16)
    out = jnp.concatenate([_dot(w_t, vb_ref[:, 0:sb_w // 2]), _dot(w_t, vb_ref[:, sb_w // 2:])],
                          axis=1)
    out = jnp.where(head_mask, out, 0.0)
    o = out[0:n_new, :]
    for hd in range(1, SB_HEADS):
        o = o + out[hd * n_new:(hd + 1) * n_new, :]
    og_ref[...] = o * sg_ref[...]


def _sb_sample_call(q, k_new, v_new, sg, bias_lanes, ck, cv, page_table, n_new):
    t_total, sb_w = q.shape
    n_seq, n_pages = page_table.shape
    assert n_new == SUBLANES and SB_HEADS * n_new == LANES
    n_lane = SB_HEADS * n_new
    rows_pad = (n_pages + 1) * PAGE_SIZE
    row = pl.BlockSpec((n_new, sb_w), lambda b, pt: (b, 0))

    def page_spec(p):
        return pl.BlockSpec((None, PAGE_SIZE, sb_w), lambda b, pt: (pt[b, p], 0, 0))

    grid_spec = pltpu.PrefetchScalarGridSpec(
        num_scalar_prefetch=1,
        grid=(n_seq,),
        in_specs=[row, row, row, row, pl.BlockSpec((1, n_lane), lambda b, pt: (0, 0))]
        + [page_spec(p) for p in range(n_pages)] + [page_spec(p) for p in range(n_pages)],
        out_specs=row,
        scratch_shapes=[
            pltpu.VMEM((rows_pad, sb_w), BF16),
            pltpu.VMEM((rows_pad, sb_w), BF16),
            pltpu.VMEM((rows_pad, n_lane), F32),
            pltpu.VMEM((n_pages * PAGE_SIZE, n_lane), F32),
            pltpu.VMEM((rows_pad, n_lane), F32),
        ],
    )
    return pl.pallas_call(
        functools.partial(_sb_sample_body, n_pages=n_pages, n_new=n_new),
        grid_spec=grid_spec,
        out_shape=jax.ShapeDtypeStruct((t_total, sb_w), F32),
        compiler_params=_params("arbitrary"),
        name="sb_attn_sample",
    )(page_table, q, k_new, v_new, sg, bias_lanes, *([ck] * n_pages), *([cv] * n_pages))


def _rope_tables(pos):
    half = RET_DK // 2
    inv_freq = ROPE_BASE ** (-jnp.arange(half, dtype=F32) / half)
    ang = pos.astype(F32)[:, None] * inv_freq[None, :]
    cos, sin = jnp.cos(ang), jnp.sin(ang)
    return jnp.concatenate([cos, cos], axis=1), jnp.concatenate([-sin, sin], axis=1)


def _sb_perm_matrix(kb):
    row = jnp.arange(kb)
    n = kb // SUBLANES
    src = (row % SUBLANES) * n + (n - 1 - row // SUBLANES)
    return (src[:, None] == jnp.arange(kb)[None, :]).astype(BF16)


def _head_slots(vals):
    slots = jnp.zeros((SB_HEADS, LANES), F32).at[:, SB_DH:SB_DH + 2].set(vals)
    return slots.reshape(1, SB_HEADS * LANES)


def _bias_slots(half_bias):
    hi = half_bias.astype(BF16).astype(F32)
    lo = (half_bias - hi).astype(BF16).astype(F32)
    return _head_slots(jnp.stack([hi, lo], axis=1))


def _trunk(x, mods, blocks_per_seq, n_seq, rope, s0, attn_fn, w, act_dtype, permuted):
    (norm_g, ret_w_in, ret_norm_g, ret_w_out, kv_norm_g, w_kv, sb_q_g, sb_k_g,
     sb_w_in, sb_w_out, ones_bd, bias_slots) = w
    n_a = ret_w_in.shape[0]
    n_b = sb_w_in.shape[0]
    seq_len = x.shape[0] // n_seq
    chunk = math.gcd(seq_len, RET_CHUNK)
    tabs = _ret_tables(chunk, max(chunk, LANES))
    states = None
    for l in range(n_a):
        q, k, v, sg = _ret_proj_call(x, mods[l], norm_g[l], ret_w_in[l], rope[0], rope[1],
                                     blocks_per_seq, act_dtype)
        og, states = _ret_core_call(q, k, v, sg, tabs, ret_norm_g[l], s0, l, n_a, states,
                                    n_seq, act_dtype)
        x = _out_proj_call(og, ret_w_out[l], x, mods[l], blocks_per_seq)
    k_gain_row = jnp.tile(sb_k_g.astype(F32), SB_HEADS).reshape(1, -1)
    perm = _sb_perm_matrix(SB_KEY_BLOCK) if permuted else None
    kv = _kv_call(x, kv_norm_g, w_kv, k_gain_row, ones_bd, perm,
                  _head_slots(jnp.ones((SB_HEADS, 2), F32)) if permuted else None, n_seq)
    for j in range(n_b):
        l = n_a + j
        q_gain_row = jnp.tile(sb_q_g[j].astype(F32) * (0.5 * SB_DH ** -0.5), SB_HEADS).reshape(1, -1)
        q, sg = _sb_proj_call(x, mods[l], norm_g[l], sb_w_in[j], q_gain_row, ones_bd,
                              bias_slots[j] if permuted else None, blocks_per_seq, act_dtype)
        og = attn_fn(j, q, sg, kv)
        x = _out_proj_call(og, sb_w_out[j], x, mods[l], blocks_per_seq)
    return x, states, kv[0], kv[1]


def kernel(x_prompt, x_sample, state_ret, cache_k, cache_v, page_table, c_prompt, c_sample,
           ada_w, ada_b, norm_g, ret_w_in, ret_norm_g, ret_w_out,
           kv_norm_g, w_kv, sb_q_g, sb_k_g, sb_w_in, sb_w_out, sb_bias):
    n_p, len_p, d = x_prompt.shape
    n_s, len_s, _ = x_sample.shape
    depth = ada_w.shape[0]
    sb_w = SB_HEADS * SB_DH
    past_len = page_table.shape[1] * PAGE_SIZE

    n_c = n_p + n_s
    c_rows = -(-n_c // 16) * 16
    c_all = jnp.concatenate([c_prompt.astype(F32), c_sample.astype(F32),
                             jnp.zeros((c_rows - n_c, d), F32)], axis=0)
    mod = _ada_call(c_all, ada_w.astype(F32), ada_b.astype(F32))
    mods_p = [mod[l, :n_p].reshape(n_p, 1, 3 * d) for l in range(depth)]
    mods_s = [jnp.repeat(mod[l, n_p:n_c], len_s, axis=0).reshape(1, n_s * len_s, 3 * d)
              for l in range(depth)]

    ones_bd = (jnp.arange(sb_w)[:, None] // SB_DH == jnp.arange(sb_w)[None, :] // SB_DH).astype(BF16)
    half_bias = 0.5 * sb_bias.astype(F32)
    weights = (norm_g.astype(F32), ret_w_in.astype(BF16), ret_norm_g.astype(F32), ret_w_out.astype(BF16),
               kv_norm_g.astype(F32), w_kv.astype(BF16), sb_q_g, sb_k_g,
               sb_w_in.astype(BF16), sb_w_out.astype(BF16), ones_bd,
               [_bias_slots(half_bias[j]) for j in range(sb_bias.shape[0])])

    rope_p = _rope_tables(jnp.arange(len_p, dtype=jnp.int32))
    tm_p = min(TOKEN_BLOCK, len_p)

    def attn_p(j, q, sg, kv):
        return _sb_prompt_call(q, kv[2], kv[3], sg, n_p)

    y_p, st_p, k_p, v_p = _trunk(x_prompt.astype(F32).reshape(n_p * len_p, d), mods_p, len_p // tm_p,
                                 n_p, rope_p, None, attn_p, weights, BF16, True)

    pos_s = past_len + jnp.arange(len_s, dtype=jnp.int32)
    rope_s = tuple(jnp.tile(t, (n_s, 1)) for t in _rope_tables(pos_s))

    n_pool = cache_k.shape[0]
    ck = cache_k.astype(F32).reshape(n_pool, PAGE_SIZE, sb_w)
    cv = cache_v.astype(F32).reshape(n_pool, PAGE_SIZE, sb_w)

    def attn_s(j, q, sg, kv):
        bias_lanes = jnp.repeat(half_bias[j], len_s).reshape(1, SB_HEADS * len_s)
        return _sb_sample_call(q, kv[0], kv[1], sg, bias_lanes, ck, cv, page_table, len_s)

    y_s, st_s, k_s, v_s = _trunk(x_sample.astype(F32).reshape(n_s * len_s, d), mods_s, None,
                                 n_s, rope_s, state_ret.astype(F32), attn_s, weights, F32, False)

    return (y_p.reshape(n_p, len_p, d).astype(x_prompt.dtype),
            y_s.reshape(n_s, len_s, d).astype(x_sample.dtype),
            st_p, st_s,
            k_p.reshape(n_p, len_p, SB_HEADS, SB_DH), v_p.reshape(n_p, len_p, SB_HEADS, SB_DH),
            k_s.reshape(n_s, len_s, SB_HEADS, SB_DH), v_s.reshape(n_s, len_s, SB_HEADS, SB_DH))
```

```python
import functools
import math

import jax
import jax.numpy as jnp
from jax import lax
from jax.experimental import pallas as pl
from jax.experimental.pallas import tpu as pltpu

F32 = jnp.float32
BF16 = jnp.bfloat16

RET_HEADS = 8
RET_DK = 128
RET_DV = 256
RET_CHUNK = 128
SB_HEADS = 16
SB_DH = 64
PAGE_SIZE = 128
ROPE_BASE = 10000.0
EPS = 1e-6

LANES = 128
SUBLANES = 8
VMEM_LIMIT_BYTES = 56 * 1024 * 1024

TOKEN_BLOCK = 512
SB_KEY_BLOCK = 256
SB_QUERY_BLOCK = 512
SB_LANE_GROUP = 256
ADA_COL_BLOCK = 512
GATHER_PAGES = 4


def _params(*sem):
    return pltpu.CompilerParams(dimension_semantics=sem, vmem_limit_bytes=VMEM_LIMIT_BYTES)


def _resident(a, n_grid):
    zeros = (0,) * a.ndim
    index_map = {1: lambda t: zeros, 2: lambda b, t: zeros}[n_grid]
    return pl.BlockSpec(a.shape, index_map, pipeline_mode=pl.Buffered(1))


def _split_bf16(a):
    hi = a.astype(BF16)
    lo = (a - hi.astype(F32)).astype(BF16)
    return hi, lo


def _dot(a, b):
    return jnp.dot(a, b, preferred_element_type=F32)


def _dot3(a, b):
    a_hi, a_lo = _split_bf16(a)
    b_hi, b_lo = _split_bf16(b)
    return _dot(a_hi, b_hi) + _dot(a_lo, b_hi) + _dot(a_hi, b_lo)


def _silu(x):
    return x / (1.0 + jnp.exp(-x))


def _ada_body(c_ref, w_ref, b_ref, o_ref):
    o_ref[...] = _dot3(_silu(c_ref[...]), w_ref[...]) + b_ref[...]


def _ada_call(c_pad, ada_w, ada_b):
    depth, d, d3 = ada_w.shape
    rows = c_pad.shape[0]
    tn = ADA_COL_BLOCK
    return pl.pallas_call(
        _ada_body,
        grid=(depth, d3 // tn),
        in_specs=[
            pl.BlockSpec((rows, d), lambda l, j: (0, 0)),
            pl.BlockSpec((None, d, tn), lambda l, j: (l, 0, j)),
            pl.BlockSpec((None, 1, tn), lambda l, j: (l, 0, j)),
        ],
        out_specs=pl.BlockSpec((None, rows, tn), lambda l, j: (l, 0, j)),
        out_shape=jax.ShapeDtypeStruct((depth, rows, d3), F32),
        compiler_params=_params("parallel", "parallel"),
        name="ada_mod",
    )(c_pad, ada_w, ada_b.reshape(depth, 1, d3))


def _modulated(x, g, m):
    d = x.shape[-1]
    r = lax.rsqrt(jnp.mean(x * x, axis=-1, keepdims=True) + EPS)
    return x * r * g * (1.0 + m[:, d:2 * d]) + m[:, :d]


def _mod_spec(mod, tm, blocks_per_seq):
    d3 = mod.shape[-1]
    if mod.shape[1] == 1:
        return pl.BlockSpec((None, 1, d3), lambda t: (t // blocks_per_seq, 0, 0))
    return pl.BlockSpec((None, tm, d3), lambda t: (0, t, 0))


def _tab_spec(tab, tm, blocks_per_seq):
    if blocks_per_seq is None:
        return pl.BlockSpec((tm, LANES), lambda t: (t, 0))
    return pl.BlockSpec((tm, LANES), lambda t: (t % blocks_per_seq, 0))


def _head_sum64(x_sq, ones_bd):
    return _dot(x_sq.astype(BF16), ones_bd)


def _ret_proj_body(x_ref, m_ref, g_ref, w_ref, cos_ref, sin_ref, q_ref, k_ref, v_ref, sg_ref):
    h = _modulated(x_ref[...], g_ref[...], m_ref[...]).astype(BF16)
    cosf = cos_ref[...]
    sinf = sin_ref[...]
    qk_w = RET_HEADS * RET_DK
    v_w = RET_HEADS * RET_DV

    def rotary(p, scale):
        outs = []
        for hd in range(RET_HEADS):
            sl = p[:, hd * RET_DK:(hd + 1) * RET_DK]
            rot = pltpu.roll(sl, RET_DK // 2, axis=1)
            o = sl * cosf + rot * sinf
            outs.append(o if scale is None else o * scale)
        return jnp.concatenate(outs, axis=1)

    q = _dot(h, w_ref[:, 0:qk_w])
    q_ref[...] = rotary(q, None).astype(q_ref.dtype)
    k = _dot(h, w_ref[:, qk_w:2 * qk_w])
    k_ref[...] = rotary(k, RET_DK ** -0.5).astype(k_ref.dtype)
    v_ref[...] = _dot(h, w_ref[:, 2 * qk_w:2 * qk_w + v_w]).astype(v_ref.dtype)
    g = _dot(h, w_ref[:, 2 * qk_w + v_w:2 * qk_w + 2 * v_w])
    sg_ref[...] = _silu(g).astype(sg_ref.dtype)


def _ret_proj_call(x, mod, g, w_bf16, cosf, sinf, blocks_per_seq, out_dtype):
    t_total, d = x.shape
    tm = min(TOKEN_BLOCK, t_total)
    qk_w = RET_HEADS * RET_DK
    v_w = RET_HEADS * RET_DV
    row = lambda w: pl.BlockSpec((tm, w), lambda t: (t, 0))
    return pl.pallas_call(
        _ret_proj_body,
        grid=(t_total // tm,),
        in_specs=[
            row(d),
            _mod_spec(mod, tm, blocks_per_seq),
            pl.BlockSpec((1, d), lambda t: (0, 0)),
            _resident(w_bf16, 1),
            _tab_spec(cosf, tm, blocks_per_seq if mod.shape[1] == 1 else None),
            _tab_spec(sinf, tm, blocks_per_seq if mod.shape[1] == 1 else None),
        ],
        out_specs=[row(qk_w), row(qk_w), row(v_w), row(v_w)],
        out_shape=[
            jax.ShapeDtypeStruct((t_total, qk_w), out_dtype),
            jax.ShapeDtypeStruct((t_total, qk_w), out_dtype),
            jax.ShapeDtypeStruct((t_total, v_w), out_dtype),
            jax.ShapeDtypeStruct((t_total, v_w), out_dtype),
        ],
        compiler_params=_params("parallel"),
        name="ret_proj",
    )(x, mod, g.reshape(1, d), w_bf16, cosf, sinf)


def _ret_core_body(*refs, chunk, chunk_pad, n_chunks, has_s0, has_stack, layer):
    q_ref, k_ref, v_ref, sg_ref, dm_ref, qd_ref, kd_ref, cd_ref, ng_ref = refs[:9]
    s0_ref = refs[9] if has_s0 else None
    og_ref, stack_ref = refs[9 + has_s0 + has_stack:]
    mxu = BF16 if chunk >= 16 else F32
    s_ref = stack_ref if has_stack else stack_ref.at[layer]

    @pl.when(pl.program_id(1) == 0)
    def _():
        if not has_stack:
            stack_ref[...] = jnp.zeros_like(stack_ref)
        if has_s0:
            s_ref[...] = s0_ref[...]
        elif has_stack:
            s_ref[...] = jnp.zeros_like(s_ref)

    def pad_rows(a):
        if chunk_pad == chunk:
            return a
        return jnp.concatenate([a, jnp.zeros((chunk_pad - chunk, a.shape[1]), a.dtype)], axis=0)

    def one_chunk(ci, carry):
        r0 = pl.multiple_of(ci * chunk, chunk)
        rows = pl.ds(r0, chunk)
        for hd in range(RET_HEADS):
            qc = q_ref[rows, hd * RET_DK:(hd + 1) * RET_DK]
            kc = pad_rows(k_ref[rows, hd * RET_DK:(hd + 1) * RET_DK])
            vc = pad_rows(v_ref[rows, hd * RET_DV:(hd + 1) * RET_DV]).astype(mxu)
            s_old = s_ref[hd]
            scores = lax.dot_general(qc.astype(mxu), kc.astype(mxu), (((1,), (1,)), ((), ())),
                                     preferred_element_type=F32) * dm_ref[hd]
            intra = _dot(scores.astype(mxu), vc)
            qdec = (qc.astype(F32) * qd_ref[hd]).astype(mxu)
            cross = _dot(qdec, s_old.astype(mxu))
            o = intra + cross
            kdec = (kc.astype(F32) * kd_ref[hd]).astype(mxu)
            s_ref[hd] = s_old * cd_ref[hd] + lax.dot_general(
                kdec, vc, (((0,), (0,)), ((), ())), preferred_element_type=F32)
            r = lax.rsqrt(jnp.mean(o * o, axis=-1, keepdims=True) + EPS)
            cols = slice(hd * RET_DV, (hd + 1) * RET_DV)
            og = o * r * ng_ref[:, cols] * sg_ref[rows, cols].astype(F32)
            og_ref[rows, cols] = og.astype(og_ref.dtype)
        return carry

    if n_chunks == 1:
        one_chunk(0, 0)
    else:
        lax.fori_loop(0, n_chunks, one_chunk, 0)


def _ret_core_call(q, k, v, sg, tabs, norm_g, s0, layer, n_layers, states, n_seq, out_dtype):
    t_total = q.shape[0]
    seq_len = t_total // n_seq
    chunk = math.gcd(seq_len, RET_CHUNK)
    chunk_pad = max(chunk, LANES)
    tm = min(TOKEN_BLOCK, seq_len)
    n_chunks = tm // chunk
    nblk = seq_len // tm
    dm, qd, kd, cd = tabs
    qk_w = RET_HEADS * RET_DK
    v_w = RET_HEADS * RET_DV
    row = lambda w: pl.BlockSpec((tm, w), lambda b, t: (b * nblk + t, 0))
    full = lambda a: _resident(a, 2)
    state_spec = pl.BlockSpec((None, None, RET_HEADS, RET_DK, RET_DV),
                              lambda b, t: (layer, b, 0, 0, 0))
    in_specs = [row(qk_w), row(qk_w), row(v_w), row(v_w), full(dm), full(qd), full(kd), full(cd),
                pl.BlockSpec((1, v_w), lambda b, t: (0, 0))]
    args = [q, k, v, sg, dm, qd, kd, cd, norm_g.reshape(1, v_w)]
    if s0 is not None:
        in_specs.append(state_spec)
        args.append(s0)
    aliases = {}
    if states is not None:
        aliases = {len(args): 1}
        in_specs.append(pl.BlockSpec(memory_space=pl.ANY))
        args.append(states)
    body = functools.partial(_ret_core_body, chunk=chunk, chunk_pad=chunk_pad, n_chunks=n_chunks,
                             has_s0=s0 is not None, has_stack=states is not None, layer=layer)
    out_state_spec = state_spec if states is not None else pl.BlockSpec(
        (n_layers, None, RET_HEADS, RET_DK, RET_DV), lambda b, t: (0, b, 0, 0, 0))
    return pl.pallas_call(
        body,
        grid=(n_seq, nblk),
        in_specs=in_specs,
        out_specs=[row(v_w), out_state_spec],
        out_shape=[
            jax.ShapeDtypeStruct((t_total, v_w), out_dtype),
            jax.ShapeDtypeStruct((n_layers, n_seq, RET_HEADS, RET_DK, RET_DV), F32),
        ],
        input_output_aliases=aliases,
        compiler_params=_params("parallel", "arbitrary"),
        name="ret_core",
    )(*args)


def _ret_tables(chunk, chunk_pad):
    hds = jnp.arange(RET_HEADS, dtype=F32)
    log_gamma = jnp.log1p(-jnp.exp2(-5.0 - hds))
    idx = jnp.arange(chunk, dtype=F32)
    diff = idx[:, None] - idx[None, :]
    dmask = jnp.where(diff[None] >= 0,
                      jnp.exp(log_gamma[:, None, None] * jnp.maximum(diff, 0.0)[None]), 0.0)
    dmask = jnp.pad(dmask, ((0, 0), (0, 0), (0, chunk_pad - chunk)))
    q_dec = jnp.exp(log_gamma[:, None] * (idx[None, :] + 1.0))
    k_dec = jnp.exp(log_gamma[:, None] * (chunk - 1.0 - idx[None, :]))
    k_dec = jnp.pad(k_dec, ((0, 0), (0, chunk_pad - chunk)))
    c_dec = jnp.exp(log_gamma * chunk)
    qd = jnp.broadcast_to(q_dec[:, :, None], (RET_HEADS, chunk, RET_DK))
    kd = jnp.broadcast_to(k_dec[:, :, None], (RET_HEADS, chunk_pad, RET_DK))
    cd = jnp.broadcast_to(c_dec[:, None, None], (RET_HEADS, 1, RET_DV))
    return dmask.astype(F32), qd.astype(F32), kd.astype(F32), cd.astype(F32)


def _out_proj_body(a_ref, w_ref, x_ref, m_ref, o_ref):
    d = x_ref.shape[-1]
    y = _dot(a_ref[...].astype(BF16), w_ref[...])
    o_ref[...] = x_ref[...] + m_ref[:, 2 * d:3 * d] * y


def _out_proj_call(a, w_bf16, x, mod, blocks_per_seq):
    t_total, d = x.shape
    tm = min(TOKEN_BLOCK, t_total)
    ka = a.shape[1]
    return pl.pallas_call(
        _out_proj_body,
        grid=(t_total // tm,),
        in_specs=[
            pl.BlockSpec((tm, ka), lambda t: (t, 0)),
            _resident(w_bf16, 1),
            pl.BlockSpec((tm, d), lambda t: (t, 0)),
            _mod_spec(mod, tm, blocks_per_seq),
        ],
        out_specs=pl.BlockSpec((tm, d), lambda t: (t, 0)),
        out_shape=jax.ShapeDtypeStruct((t_total, d), F32),
        compiler_params=_params("parallel"),
        name="out_proj",
    )(a, w_bf16, x, mod)


def _kv_body(*refs, permuted):
    if permuted:
        (x_ref, g_ref, w_ref, kg_ref, ones_ref, perm_ref, slots_ref,
         k_ref, v_ref, kp_ref, vt_ref) = refs
    else:
        x_ref, g_ref, w_ref, kg_ref, ones_ref, k_ref, v_ref = refs
    x = x_ref[...]
    sb_w = SB_HEADS * SB_DH
    r = lax.rsqrt(jnp.mean(x * x, axis=-1, keepdims=True) + EPS)
    xn = (x * r * g_ref[...]).astype(BF16)
    k = _dot(xn, w_ref[:, :sb_w])
    v = _dot(xn, w_ref[:, sb_w:])
    ms = _head_sum64(k * k, ones_ref[...]) * (1.0 / SB_DH)
    k = k * lax.rsqrt(ms + EPS) * kg_ref[...]
    k_ref[...] = k
    v_ref[...] = v
    if permuted:
        kb = SB_KEY_BLOCK
        perm = perm_ref[...]
        cols = []
        for j in range(x.shape[0] // kb):
            rows = slice(j * kb, (j + 1) * kb)
            kp_ref[rows, :] = _expand_heads(_dot(perm, k[rows].astype(BF16)),
                                            slots_ref[...]).astype(BF16)
            cols.append(_dot(perm, v[rows].astype(BF16)).T.astype(BF16))
        vt_ref[0] = jnp.concatenate(cols, axis=1)


def _kv_call(x, g, w_bf16, k_gain_row, ones_bd, perm, one_slots, n_seq):
    t_total, d = x.shape
    tm = min(TOKEN_BLOCK, t_total)
    sb_w = SB_HEADS * SB_DH
    row = lambda w: pl.BlockSpec((tm, w), lambda t: (t, 0))
    const = lambda a: _resident(a, 1)
    in_specs = [row(d), pl.BlockSpec((1, d), lambda t: (0, 0)), const(w_bf16),
                const(k_gain_row), const(ones_bd)]
    args = [x, g.reshape(1, d), w_bf16, k_gain_row, ones_bd]
    out_specs = [row(sb_w), row(sb_w)]
    out_shape = [jax.ShapeDtypeStruct((t_total, sb_w), F32), jax.ShapeDtypeStruct((t_total, sb_w), F32)]
    if perm is not None:
        sup = SB_QUERY_BLOCK
        assert sup % tm == 0 and tm % SB_KEY_BLOCK == 0
        per_sup = sup // tm
        kp_w = SB_HEADS * LANES
        in_specs += [const(perm), const(one_slots)]
        args += [perm, one_slots]
        out_specs += [row(kp_w),
                      pl.BlockSpec((1, sb_w, tm), lambda t: (t // per_sup, 0, t % per_sup))]
        out_shape += [jax.ShapeDtypeStruct((t_total, kp_w), BF16),
                      jax.ShapeDtypeStruct((t_total // sup, sb_w, sup), BF16)]
    return pl.pallas_call(
        functools.partial(_kv_body, permuted=perm is not None),
        grid=(t_total // tm,),
        in_specs=in_specs,
        out_specs=out_specs,
        out_shape=out_shape,
        compiler_params=_params("parallel"),
        name="shared_kv",
    )(*args)


def _expand_heads(x, slots):
    lane = lax.broadcasted_iota(jnp.int32, (x.shape[0], LANES), 1)
    cols = []
    for hd in range(SB_HEADS):
        col = x[:, (hd // 2) * LANES:(hd // 2 + 1) * LANES]
        if hd % 2:
            col = pltpu.roll(col, SB_DH, axis=1)
        cols.append(jnp.where(lane < SB_DH, col, slots[:, hd * LANES:(hd + 1) * LANES]))
    return jnp.concatenate(cols, axis=1)


def _sb_proj_body(*refs, expand):
    if expand:
        x_ref, m_ref, g_ref, w_ref, qg_ref, ones_ref, slots_ref, q_ref, sg_ref = refs
    else:
        x_ref, m_ref, g_ref, w_ref, qg_ref, ones_ref, q_ref, sg_ref = refs
    h = _modulated(x_ref[...], g_ref[...], m_ref[...]).astype(BF16)
    sb_w = SB_HEADS * SB_DH
    q = _dot(h, w_ref[:, :sb_w])
    ms = _head_sum64(q * q, ones_ref[...]) * (1.0 / SB_DH)
    q = q * lax.rsqrt(ms + EPS) * qg_ref[...]
    if expand:
        q = _expand_heads(q, slots_ref[...])
    q_ref[...] = q.astype(q_ref.dtype)
    sg_ref[...] = _silu(_dot(h, w_ref[:, sb_w:])).astype(sg_ref.dtype)


def _sb_proj_call(x, mod, g, w_bf16, q_gain_row, ones_bd, bias_slots, blocks_per_seq, out_dtype):
    t_total, d = x.shape
    tm = min(TOKEN_BLOCK, t_total)
    sb_w = SB_HEADS * SB_DH
    row = lambda w: pl.BlockSpec((tm, w), lambda t: (t, 0))
    const = lambda a: _resident(a, 1)
    expand = bias_slots is not None
    q_w = SB_HEADS * LANES if expand else sb_w
    in_specs = [row(d), _mod_spec(mod, tm, blocks_per_seq), pl.BlockSpec((1, d), lambda t: (0, 0)),
                const(w_bf16), const(q_gain_row), const(ones_bd)]
    args = [x, mod, g.reshape(1, d), w_bf16, q_gain_row, ones_bd]
    if expand:
        in_specs.append(const(bias_slots))
        args.append(bias_slots)
    return pl.pallas_call(
        functools.partial(_sb_proj_body, expand=expand),
        grid=(t_total // tm,),
        in_specs=in_specs,
        out_specs=[row(q_w), row(sb_w)],
        out_shape=[jax.ShapeDtypeStruct((t_total, q_w), out_dtype),
                   jax.ShapeDtypeStruct((t_total, sb_w), out_dtype)],
        compiler_params=_params("parallel"),
        name="sb_proj",
    )(*args)


def _sb_beta_keep(u, mask):
    half_t = 0.5 * jnp.tanh(u)
    keep, beta = 0.5 - half_t, 0.5 + half_t
    if mask is not None:
        keep, beta = jnp.where(mask, keep, 1.0), jnp.where(mask, beta, 0.0)
    return beta, keep


def _sublane_suffix_products(x):
    sub = lax.broadcasted_iota(jnp.int32, x.shape, 0)
    inc = x
    for step in (1, 2, 4):
        shifted = pltpu.roll(inc, SUBLANES - step, axis=0)
        inc = inc * jnp.where(sub < SUBLANES - step, shifted, 1.0)
    after = jnp.where(sub < SUBLANES - 1, pltpu.roll(inc, SUBLANES - 1, axis=0), 1.0)
    return inc, after


def _sb_blocks(us, carry, masks):
    kb, nq = us[0].shape
    n = kb // SUBLANES
    runs = [None] * len(us)
    local = [[None] * n for _ in us]
    for i in range(n):
        rows = slice(i * SUBLANES, (i + 1) * SUBLANES)
        for b, (u, mask) in enumerate(zip(us, masks)):
            h_i = 0.5 * jnp.tanh(u[rows, :])
            if mask is not None:
                h_i = jnp.where(mask[rows, :], h_i, -0.5)
            beta_i, keep_i = 0.5 + h_i, 0.5 - h_i
            local[b][i] = beta_i if runs[b] is None else beta_i * runs[b]
            runs[b] = keep_i if runs[b] is None else runs[b] * keep_i
    ws = []
    for b in range(len(us)):
        inc, after = _sublane_suffix_products(runs[b])
        scale = carry * after
        carry = carry * jnp.broadcast_to(inc[0:1, :], (SUBLANES, nq))
        ws.append(jnp.concatenate([local[b][i] * scale for i in range(n)], axis=0))
    return ws, carry


def _sb_block(z_t, carry, mask):
    ws, carry = _sb_blocks([z_t], carry, [mask])
    return ws[0], carry


def _sb_key_offsets(kb, nq):
    row = lax.broadcasted_iota(jnp.int32, (kb, nq), 0)
    n = kb // SUBLANES
    return (row & (SUBLANES - 1)) * n + (n - 1 - (row >> 3))


def _sb_prompt_body(q_ref, kp_ref, vt_ref, sg_ref, og_ref, acc_ref):
    qi = pl.program_id(2)
    qb = q_ref.shape[0]
    kb = SB_KEY_BLOCK
    ratio = qb // kb
    lg = SB_LANE_GROUP
    units = [(j, h) for j in range(2) for h in range(qb // lg)]
    q_unit = [q_ref[h * lg:(h + 1) * lg, j * LANES:(j + 1) * LANES] for j, h in units]
    order = list(range(ratio - 1, -1, -1))

    def logits(ks, u):
        r0 = pl.multiple_of(ks * qb, qb)
        j = units[u][0]
        return lax.dot_general(kp_ref[pl.ds(r0, qb), j * LANES:(j + 1) * LANES], q_unit[u],
                               (((1,), (1,)), ((), ())), preferred_element_type=F32)

    def weights(z_t, carry, masks):
        live = [c for c in order if masks is None or masks[c] is not None]
        ws, carry = _sb_blocks([z_t[c * kb:(c + 1) * kb, :] for c in live], carry,
                               [None if masks is None else masks[c] for c in live])
        by_block = dict(zip(live, ws))
        rows = [by_block[c] if c in by_block else jnp.zeros((kb, lg), F32) for c in range(ratio)]
        return jnp.concatenate(rows, axis=0).astype(BF16), carry

    def accumulate(ks, u, w):
        j, h = units[u]
        acc_ref[j, :, h * lg:(h + 1) * lg] += _dot(vt_ref[ks, j * SB_DH:(j + 1) * SB_DH, :], w)

    def super_block(ks, ks_next, z_first, w_last, carries, masks):
        z_t, w_prev = z_first, w_last
        out = []
        for u in range(len(units)):
            z_ahead = logits(ks, u + 1) if u + 1 < len(units) else logits(ks_next, 0)
            if u > 0:
                accumulate(ks, u - 1, w_prev)
            elif w_prev is not None:
                accumulate(ks + 1, len(units) - 1, w_prev)
            w_prev, carry = weights(z_t, carries[u], None if masks is None else masks[units[u][1]])
            out.append(carry)
            z_t = z_ahead
        return z_t, w_prev, tuple(out)

    acc_ref[...] = jnp.zeros_like(acc_ref)
    key_off = _sb_key_offsets(kb, lg)
    q_idx = lax.broadcasted_iota(jnp.int32, (kb, lg), 1)
    masks = [[key_off + c * kb < q_idx + h * lg if c * kb < (h + 1) * lg - 1 else None
              for c in range(ratio)] for h in range(qb // lg)]
    ones = jnp.ones((SUBLANES, lg), F32)
    state = super_block(qi, jnp.maximum(qi - 1, 0), logits(qi, 0), None, (ones,) * len(units), masks)

    def trips(first, count):
        def body(it, state):
            ks = first - count * it
            for c in range(count):
                state = super_block(ks - c, jnp.maximum(ks - c - 1, 0), *state, None)
            return state
        return body

    odd = qi & 1
    state = lax.fori_loop(0, odd, trips(qi - 1, 1), state)
    _, w_last, _ = lax.fori_loop(0, qi >> 1, trips(qi - 1 - odd, 2), state)
    accumulate(0, len(units) - 1, w_last)
    o = jnp.concatenate([acc_ref[0], acc_ref[1]], axis=0).T
    og_ref[...] = (o * sg_ref[...].astype(F32)).astype(og_ref.dtype)


def _sb_prompt_call(q, kp, vt, sg, n_seq):
    t_total, sb_w = sg.shape
    seq_len = t_total // n_seq
    qb = SB_QUERY_BLOCK
    assert seq_len % qb == 0
    n_q = seq_len // qb
    n_hp = sb_w // LANES
    return pl.pallas_call(
        _sb_prompt_body,
        grid=(n_seq, n_hp, n_q),
        in_specs=[
            pl.BlockSpec((qb, 2 * LANES), lambda b, hp, qi: (b * n_q + qi, hp)),
            pl.BlockSpec((seq_len, 2 * LANES), lambda b, hp, qi: (b, hp)),
            pl.BlockSpec((n_q, LANES, qb), lambda b, hp, qi: (b, hp, 0)),
            pl.BlockSpec((qb, LANES), lambda b, hp, qi: (b * n_q + qi, hp)),
        ],
        out_specs=pl.BlockSpec((qb, LANES), lambda b, hp, qi: (b * n_q + qi, hp)),
        scratch_shapes=[pltpu.VMEM((2, SB_DH, qb), F32)],
        out_shape=jax.ShapeDtypeStruct((t_total, sb_w), BF16),
        compiler_params=_params("parallel", "parallel", "arbitrary"),
        name="sb_attn_prompt",
    )(q, kp, vt, sg)


def _sb_sample_body(q_ref, kn_ref, vn_ref, sg_ref, bias_ref, kg_ref, vg_ref, og_ref,
                    z_ref, inc_ref, w_ref, *, n_new):
    past, sb_w = kg_ref.shape
    n_lane = SB_HEADS * n_new
    pad_rows = PAGE_SIZE - n_new

    q = q_ref[...]
    q_rows = jnp.concatenate([q] * SB_HEADS, axis=0)
    row_head = lax.broadcasted_iota(jnp.int32, (n_lane, sb_w), 0) // n_new
    lane_head = lax.broadcasted_iota(jnp.int32, (n_lane, sb_w), 1) // SB_DH
    head_mask = row_head == lane_head
    q_bd = jnp.where(head_mask, q_rows, 0.0).T.astype(BF16)

    half = past // 2
    z_ref[0:half, :] = _dot(kg_ref[0:half, :], q_bd) + bias_ref[...]
    z_ref[half:, :] = _dot(kg_ref[half:, :], q_bd) + bias_ref[...]
    zeros_pad = jnp.zeros((pad_rows, sb_w), F32)
    k_new = jnp.concatenate([kn_ref[...], zeros_pad], axis=0).astype(BF16)
    v_new = jnp.concatenate([vn_ref[...], zeros_pad], axis=0).astype(BF16)
    z_new = (_dot(k_new, q_bd) + bias_ref[...])[0:n_new, :]

    key_idx = lax.broadcasted_iota(jnp.int32, (n_new, n_lane), 0)
    q_idx = lax.broadcasted_iota(jnp.int32, (n_new, n_lane), 1) % n_new
    carry = jnp.ones((SUBLANES, n_lane), F32)
    w_new, carry = _sb_block(z_new, carry, key_idx < q_idx)
    w_new = jnp.concatenate([w_new, jnp.zeros((pad_rows, n_lane), F32)], axis=0)

    def local(it, _):
        rows = pl.ds(pl.multiple_of(it * SUBLANES, SUBLANES), SUBLANES)
        beta, keep = _sb_beta_keep(z_ref[rows, :], None)
        inc, after = _sublane_suffix_products(keep)
        z_ref[rows, :] = beta * after
        inc_ref[rows, :] = inc
        return 0

    lax.fori_loop(0, past // SUBLANES, local, 0, unroll=8)

    def scan(it, carry):
        rows = pl.ds(pl.multiple_of(past - SUBLANES * (it + 1), SUBLANES), SUBLANES)
        w_ref[rows, :] = z_ref[rows, :] * carry
        return carry * jnp.broadcast_to(inc_ref[rows, :][0:1, :], (SUBLANES, n_lane))

    lax.fori_loop(0, past // SUBLANES, scan, carry, unroll=8)

    w_t = w_ref[...].T.astype(BF16)
    out = jnp.concatenate([_dot(w_t, vg_ref[:, 0:sb_w // 2]), _dot(w_t, vg_ref[:, sb_w // 2:])],
                          axis=1)
    out = out + _dot(w_new.T.astype(BF16), v_new)
    out = jnp.where(head_mask, out, 0.0)
    o = out[0:n_new, :]
    for hd in range(1, SB_HEADS):
        o = o + out[hd * n_new:(hd + 1) * n_new, :]
    og_ref[...] = o * sg_ref[...]


def _sb_sample_call(q, k_new, v_new, sg, bias_lanes, kg, vg, n_new):
    t_total, sb_w = q.shape
    n_seq, past, _ = kg.shape
    assert n_new == SUBLANES and SB_HEADS * n_new == LANES
    n_lane = SB_HEADS * n_new
    row = pl.BlockSpec((n_new, sb_w), lambda b: (b, 0))
    seq = pl.BlockSpec((None, past, sb_w), lambda b: (b, 0, 0))
    return pl.pallas_call(
        functools.partial(_sb_sample_body, n_new=n_new),
        grid=(n_seq,),
        in_specs=[row, row, row, row, pl.BlockSpec((1, n_lane), lambda b: (0, 0)), seq, seq],
        out_specs=row,
        scratch_shapes=[pltpu.VMEM((past, n_lane), F32)] * 3,
        out_shape=jax.ShapeDtypeStruct((t_total, sb_w), F32),
        compiler_params=_params("arbitrary"),
        name="sb_attn_sample",
    )(q, k_new, v_new, sg, bias_lanes, kg, vg)


def _pool_gather_body(pt_ref, *refs, n_in):
    del pt_ref
    k_in, v_in = refs[:n_in], refs[n_in:2 * n_in]
    kg_ref, vg_ref = refs[2 * n_in:]
    sb_w = SB_HEADS * SB_DH
    for p in range(n_in):
        rows = slice(p * PAGE_SIZE, (p + 1) * PAGE_SIZE)
        kg_ref[rows, :] = k_in[p][...].reshape(PAGE_SIZE, sb_w).astype(BF16)
        vg_ref[rows, :] = v_in[p][...].reshape(PAGE_SIZE, sb_w).astype(BF16)


def _pool_gather_call(cache_k, cache_v, page_table):
    n_seq, n_pages = page_table.shape
    sb_w = SB_HEADS * SB_DH
    n_in = math.gcd(n_pages, GATHER_PAGES)

    def page_spec(p):
        return pl.BlockSpec((None, PAGE_SIZE, SB_HEADS, SB_DH),
                            lambda b, g, pt: (pt[b, g * n_in + p], 0, 0, 0))

    out_spec = pl.BlockSpec((None, n_in * PAGE_SIZE, sb_w), lambda b, g, pt: (b, g, 0))
    out_sds = jax.ShapeDtypeStruct((n_seq, n_pages * PAGE_SIZE, sb_w), BF16)
    grid_spec = pltpu.PrefetchScalarGridSpec(
        num_scalar_prefetch=1,
        grid=(n_seq, n_pages // n_in),
        in_specs=[page_spec(p) for p in range(n_in)] * 2,
        out_specs=[out_spec, out_spec],
    )
    return pl.pallas_call(
        functools.partial(_pool_gather_body, n_in=n_in),
        grid_spec=grid_spec,
        out_shape=[out_sds, out_sds],
        compiler_params=_params("parallel", "parallel"),
        name="pool_gather",
    )(page_table, *([cache_k] * n_in), *([cache_v] * n_in))


def _rope_tables(pos):
    half = RET_DK // 2
    inv_freq = ROPE_BASE ** (-jnp.arange(half, dtype=F32) / half)
    ang = pos.astype(F32)[:, None] * inv_freq[None, :]
    cos, sin = jnp.cos(ang), jnp.sin(ang)
    return jnp.concatenate([cos, cos], axis=1), jnp.concatenate([-sin, sin], axis=1)


def _sb_perm_matrix(kb):
    row = jnp.arange(kb)
    n = kb // SUBLANES
    src = (row % SUBLANES) * n + (n - 1 - row // SUBLANES)
    return (src[:, None] == jnp.arange(kb)[None, :]).astype(BF16)


def _head_slots(vals):
    slots = jnp.zeros((SB_HEADS, LANES), F32).at[:, SB_DH:SB_DH + 2].set(vals)
    return slots.reshape(1, SB_HEADS * LANES)


def _bias_slots(half_bias):
    hi = half_bias.astype(BF16).astype(F32)
    lo = (half_bias - hi).astype(BF16).astype(F32)
    return _head_slots(jnp.stack([hi, lo], axis=1))


def _trunk(x, mods, blocks_per_seq, n_seq, rope, s0, attn_fn, w, act_dtype, permuted):
    (norm_g, ret_w_in, ret_norm_g, ret_w_out, kv_norm_g, w_kv, sb_q_g, sb_k_g,
     sb_w_in, sb_w_out, ones_bd, bias_slots) = w
    n_a = ret_w_in.shape[0]
    n_b = sb_w_in.shape[0]
    seq_len = x.shape[0] // n_seq
    chunk = math.gcd(seq_len, RET_CHUNK)
    tabs = _ret_tables(chunk, max(chunk, LANES))
    states = None
    for l in range(n_a):
        q, k, v, sg = _ret_proj_call(x, mods[l], norm_g[l], ret_w_in[l], rope[0], rope[1],
                                     blocks_per_seq, act_dtype)
        og, states = _ret_core_call(q, k, v, sg, tabs, ret_norm_g[l], s0, l, n_a, states,
                                    n_seq, act_dtype)
        x = _out_proj_call(og, ret_w_out[l], x, mods[l], blocks_per_seq)
    k_gain_row = jnp.tile(sb_k_g.astype(F32), SB_HEADS).reshape(1, -1)
    perm = _sb_perm_matrix(SB_KEY_BLOCK) if permuted else None
    kv = _kv_call(x, kv_norm_g, w_kv, k_gain_row, ones_bd, perm,
                  _head_slots(jnp.ones((SB_HEADS, 2), F32)) if permuted else None, n_seq)
    for j in range(n_b):
        l = n_a + j
        q_gain_row = jnp.tile(sb_q_g[j].astype(F32) * (0.5 * SB_DH ** -0.5), SB_HEADS).reshape(1, -1)
        q, sg = _sb_proj_call(x, mods[l], norm_g[l], sb_w_in[j], q_gain_row, ones_bd,
                              bias_slots[j] if permuted else None, blocks_per_seq, act_dtype)
        og = attn_fn(j, q, sg, kv)
        x = _out_proj_call(og, sb_w_out[j], x, mods[l], blocks_per_seq)
    return x, states, kv[0], kv[1]


def kernel(x_prompt, x_sample, state_ret, cache_k, cache_v, page_table, c_prompt, c_sample,
           ada_w, ada_b, norm_g, ret_w_in, ret_norm_g, ret_w_out,
           kv_norm_g, w_kv, sb_q_g, sb_k_g, sb_w_in, sb_w_out, sb_bias):
    n_p, len_p, d = x_prompt.shape
    n_s, len_s, _ = x_sample.shape
    depth = ada_w.shape[0]
    sb_w = SB_HEADS * SB_DH
    past_len = page_table.shape[1] * PAGE_SIZE

    n_c = n_p + n_s
    c_rows = -(-n_c // 16) * 16
    c_all = jnp.concatenate([c_prompt.astype(F32), c_sample.astype(F32),
                             jnp.zeros((c_rows - n_c, d), F32)], axis=0)
    mod = _ada_call(c_all, ada_w.astype(F32), ada_b.astype(F32))
    mods_p = [mod[l, :n_p].reshape(n_p, 1, 3 * d) for l in range(depth)]
    mods_s = [jnp.repeat(mod[l, n_p:n_c], len_s, axis=0).reshape(1, n_s * len_s, 3 * d)
              for l in range(depth)]

    ones_bd = (jnp.arange(sb_w)[:, None] // SB_DH == jnp.arange(sb_w)[None, :] // SB_DH).astype(BF16)
    half_bias = 0.5 * sb_bias.astype(F32)
    weights = (norm_g.astype(F32), ret_w_in.astype(BF16), ret_norm_g.astype(F32), ret_w_out.astype(BF16),
               kv_norm_g.astype(F32), w_kv.astype(BF16), sb_q_g, sb_k_g,
               sb_w_in.astype(BF16), sb_w_out.astype(BF16), ones_bd,
               [_bias_slots(half_bias[j]) for j in range(sb_bias.shape[0])])

    rope_p = _rope_tables(jnp.arange(len_p, dtype=jnp.int32))
    tm_p = min(TOKEN_BLOCK, len_p)

    def attn_p(j, q, sg, kv):
        return _sb_prompt_call(q, kv[2], kv[3], sg, n_p)

    y_p, st_p, k_p, v_p = _trunk(x_prompt.astype(F32).reshape(n_p * len_p, d), mods_p, len_p // tm_p,
                                 n_p, rope_p, None, attn_p, weights, BF16, True)

    pos_s = past_len + jnp.arange(len_s, dtype=jnp.int32)
    rope_s = tuple(jnp.tile(t, (n_s, 1)) for t in _rope_tables(pos_s))

    kg, vg = _pool_gather_call(cache_k.astype(F32), cache_v.astype(F32), page_table)

    def attn_s(j, q, sg, kv):
        bias_lanes = jnp.repeat(half_bias[j], len_s).reshape(1, SB_HEADS * len_s)
        return _sb_sample_call(q, kv[0], kv[1], sg, bias_lanes, kg, vg, len_s)

    y_s, st_s, k_s, v_s = _trunk(x_sample.astype(F32).reshape(n_s * len_s, d), mods_s, None,
                                 n_s, rope_s, state_ret.astype(F32), attn_s, weights, F32, False)

    return (y_p.reshape(n_p, len_p, d).astype(x_prompt.dtype),
            y_s.reshape(n_s, len_s, d).astype(x_sample.dtype),
            st_p, st_s,
            k_p.reshape(n_p, len_p, SB_HEADS, SB_DH), v_p.reshape(n_p, len_p, SB_HEADS, SB_DH),
            k_s.reshape(n_s, len_s, SB_HEADS, SB_DH), v_s.reshape(n_s, len_s, SB_HEADS, SB_DH))
```

```python
import functools
import math

import jax
import jax.numpy as jnp
from jax import lax
from jax.experimental import pallas as pl
from jax.experimental.pallas import tpu as pltpu

F32 = jnp.float32
BF16 = jnp.bfloat16

RET_HEADS = 8
RET_DK = 128
RET_DV = 256
RET_CHUNK = 128
SB_HEADS = 16
SB_DH = 64
PAGE_SIZE = 128
ROPE_BASE = 10000.0
EPS = 1e-6

LANES = 128
SUBLANES = 8
VMEM_LIMIT_BYTES = 56 * 1024 * 1024

TOKEN_BLOCK = 512
SB_KEY_BLOCK = 256
SB_QUERY_BLOCK = 512
SB_LANE_GROUP = 256
ADA_COL_BLOCK = 512


def _params(*sem):
    return pltpu.CompilerParams(dimension_semantics=sem, vmem_limit_bytes=VMEM_LIMIT_BYTES)


def _resident(a, n_grid):
    zeros = (0,) * a.ndim
    index_map = {1: lambda t: zeros, 2: lambda b, t: zeros}[n_grid]
    return pl.BlockSpec(a.shape, index_map, pipeline_mode=pl.Buffered(1))


def _split_bf16(a):
    hi = a.astype(BF16)
    lo = (a - hi.astype(F32)).astype(BF16)
    return hi, lo


def _dot(a, b):
    return jnp.dot(a, b, preferred_element_type=F32)


def _dot3(a, b):
    a_hi, a_lo = _split_bf16(a)
    b_hi, b_lo = _split_bf16(b)
    return _dot(a_hi, b_hi) + _dot(a_lo, b_hi) + _dot(a_hi, b_lo)


def _silu(x):
    return x / (1.0 + jnp.exp(-x))


def _ada_body(c_ref, w_ref, b_ref, o_ref):
    o_ref[...] = _dot3(_silu(c_ref[...]), w_ref[...]) + b_ref[...]


def _ada_call(c_pad, ada_w, ada_b):
    depth, d, d3 = ada_w.shape
    rows = c_pad.shape[0]
    tn = ADA_COL_BLOCK
    return pl.pallas_call(
        _ada_body,
        grid=(depth, d3 // tn),
        in_specs=[
            pl.BlockSpec((rows, d), lambda l, j: (0, 0)),
            pl.BlockSpec((None, d, tn), lambda l, j: (l, 0, j)),
            pl.BlockSpec((None, 1, tn), lambda l, j: (l, 0, j)),
        ],
        out_specs=pl.BlockSpec((None, rows, tn), lambda l, j: (l, 0, j)),
        out_shape=jax.ShapeDtypeStruct((depth, rows, d3), F32),
        compiler_params=_params("parallel", "parallel"),
        name="ada_mod",
    )(c_pad, ada_w, ada_b.reshape(depth, 1, d3))


def _modulated(x, g, m):
    d = x.shape[-1]
    r = lax.rsqrt(jnp.mean(x * x, axis=-1, keepdims=True) + EPS)
    return x * r * g * (1.0 + m[:, d:2 * d]) + m[:, :d]


def _mod_spec(mod, tm, blocks_per_seq):
    d3 = mod.shape[-1]
    if mod.shape[1] == 1:
        return pl.BlockSpec((None, 1, d3), lambda t: (t // blocks_per_seq, 0, 0))
    return pl.BlockSpec((None, tm, d3), lambda t: (0, t, 0))


def _tab_spec(tab, tm, blocks_per_seq):
    if blocks_per_seq is None:
        return pl.BlockSpec((tm, LANES), lambda t: (t, 0))
    return pl.BlockSpec((tm, LANES), lambda t: (t % blocks_per_seq, 0))


def _head_sum64(x_sq, ones_bd):
    return _dot(x_sq.astype(BF16), ones_bd)


def _ret_proj_body(x_ref, m_ref, g_ref, w_ref, cos_ref, sin_ref, q_ref, k_ref, v_ref, sg_ref):
    h = _modulated(x_ref[...], g_ref[...], m_ref[...]).astype(BF16)
    cosf = cos_ref[...]
    sinf = sin_ref[...]
    qk_w = RET_HEADS * RET_DK
    v_w = RET_HEADS * RET_DV

    def rotary(p, scale):
        outs = []
        for hd in range(RET_HEADS):
            sl = p[:, hd * RET_DK:(hd + 1) * RET_DK]
            rot = pltpu.roll(sl, RET_DK // 2, axis=1)
            o = sl * cosf + rot * sinf
            outs.append(o if scale is None else o * scale)
        return jnp.concatenate(outs, axis=1)

    q = _dot(h, w_ref[:, 0:qk_w])
    q_ref[...] = rotary(q, None).astype(q_ref.dtype)
    k = _dot(h, w_ref[:, qk_w:2 * qk_w])
    k_ref[...] = rotary(k, RET_DK ** -0.5).astype(k_ref.dtype)
    v_ref[...] = _dot(h, w_ref[:, 2 * qk_w:2 * qk_w + v_w]).astype(v_ref.dtype)
    g = _dot(h, w_ref[:, 2 * qk_w + v_w:2 * qk_w + 2 * v_w])
    sg_ref[...] = _silu(g).astype(sg_ref.dtype)


def _ret_proj_call(x, mod, g, w_bf16, cosf, sinf, blocks_per_seq, out_dtype):
    t_total, d = x.shape
    tm = min(TOKEN_BLOCK, t_total)
    qk_w = RET_HEADS * RET_DK
    v_w = RET_HEADS * RET_DV
    row = lambda w: pl.BlockSpec((tm, w), lambda t: (t, 0))
    return pl.pallas_call(
        _ret_proj_body,
        grid=(t_total // tm,),
        in_specs=[
            row(d),
            _mod_spec(mod, tm, blocks_per_seq),
            pl.BlockSpec((1, d), lambda t: (0, 0)),
            _resident(w_bf16, 1),
            _tab_spec(cosf, tm, blocks_per_seq if mod.shape[1] == 1 else None),
            _tab_spec(sinf, tm, blocks_per_seq if mod.shape[1] == 1 else None),
        ],
        out_specs=[row(qk_w), row(qk_w), row(v_w), row(v_w)],
        out_shape=[
            jax.ShapeDtypeStruct((t_total, qk_w), out_dtype),
            jax.ShapeDtypeStruct((t_total, qk_w), out_dtype),
            jax.ShapeDtypeStruct((t_total, v_w), out_dtype),
            jax.ShapeDtypeStruct((t_total, v_w), out_dtype),
        ],
        compiler_params=_params("parallel"),
        name="ret_proj",
    )(x, mod, g.reshape(1, d), w_bf16, cosf, sinf)


def _ret_core_body(*refs, chunk, chunk_pad, n_chunks, has_s0, has_stack, layer):
    q_ref, k_ref, v_ref, sg_ref, dm_ref, qd_ref, kd_ref, cd_ref, ng_ref = refs[:9]
    s0_ref = refs[9] if has_s0 else None
    og_ref, stack_ref = refs[9 + has_s0 + has_stack:]
    mxu = BF16 if chunk >= 16 else F32
    s_ref = stack_ref if has_stack else stack_ref.at[layer]

    @pl.when(pl.program_id(1) == 0)
    def _():
        if not has_stack:
            stack_ref[...] = jnp.zeros_like(stack_ref)
        if has_s0:
            s_ref[...] = s0_ref[...]
        elif has_stack:
            s_ref[...] = jnp.zeros_like(s_ref)

    def pad_rows(a):
        if chunk_pad == chunk:
            return a
        return jnp.concatenate([a, jnp.zeros((chunk_pad - chunk, a.shape[1]), a.dtype)], axis=0)

    def one_chunk(ci, carry):
        r0 = pl.multiple_of(ci * chunk, chunk)
        rows = pl.ds(r0, chunk)
        for hd in range(RET_HEADS):
            qc = q_ref[rows, hd * RET_DK:(hd + 1) * RET_DK]
            kc = pad_rows(k_ref[rows, hd * RET_DK:(hd + 1) * RET_DK])
            vc = pad_rows(v_ref[rows, hd * RET_DV:(hd + 1) * RET_DV]).astype(mxu)
            s_old = s_ref[hd]
            scores = lax.dot_general(qc.astype(mxu), kc.astype(mxu), (((1,), (1,)), ((), ())),
                                     preferred_element_type=F32) * dm_ref[hd]
            intra = _dot(scores.astype(mxu), vc)
            qdec = (qc.astype(F32) * qd_ref[hd]).astype(mxu)
            cross = _dot(qdec, s_old.astype(mxu))
            o = intra + cross
            kdec = (kc.astype(F32) * kd_ref[hd]).astype(mxu)
            s_ref[hd] = s_old * cd_ref[hd] + lax.dot_general(
                kdec, vc, (((0,), (0,)), ((), ())), preferred_element_type=F32)
            r = lax.rsqrt(jnp.mean(o * o, axis=-1, keepdims=True) + EPS)
            cols = slice(hd * RET_DV, (hd + 1) * RET_DV)
            og = o * r * ng_ref[:, cols] * sg_ref[rows, cols].astype(F32)
            og_ref[rows, cols] = og.astype(og_ref.dtype)
        return carry

    if n_chunks == 1:
        one_chunk(0, 0)
    else:
        lax.fori_loop(0, n_chunks, one_chunk, 0)


def _ret_core_call(q, k, v, sg, tabs, norm_g, s0, layer, n_layers, states, n_seq, out_dtype):
    t_total = q.shape[0]
    seq_len = t_total // n_seq
    chunk = math.gcd(seq_len, RET_CHUNK)
    chunk_pad = max(chunk, LANES)
    tm = min(TOKEN_BLOCK, seq_len)
    n_chunks = tm // chunk
    nblk = seq_len // tm
    dm, qd, kd, cd = tabs
    qk_w = RET_HEADS * RET_DK
    v_w = RET_HEADS * RET_DV
    row = lambda w: pl.BlockSpec((tm, w), lambda b, t: (b * nblk + t, 0))
    full = lambda a: _resident(a, 2)
    state_spec = pl.BlockSpec((None, None, RET_HEADS, RET_DK, RET_DV),
                              lambda b, t: (layer, b, 0, 0, 0))
    in_specs = [row(qk_w), row(qk_w), row(v_w), row(v_w), full(dm), full(qd), full(kd), full(cd),
                pl.BlockSpec((1, v_w), lambda b, t: (0, 0))]
    args = [q, k, v, sg, dm, qd, kd, cd, norm_g.reshape(1, v_w)]
    if s0 is not None:
        in_specs.append(state_spec)
        args.append(s0)
    aliases = {}
    if states is not None:
        aliases = {len(args): 1}
        in_specs.append(pl.BlockSpec(memory_space=pl.ANY))
        args.append(states)
    body = functools.partial(_ret_core_body, chunk=chunk, chunk_pad=chunk_pad, n_chunks=n_chunks,
                             has_s0=s0 is not None, has_stack=states is not None, layer=layer)
    out_state_spec = state_spec if states is not None else pl.BlockSpec(
        (n_layers, None, RET_HEADS, RET_DK, RET_DV), lambda b, t: (0, b, 0, 0, 0))
    return pl.pallas_call(
        body,
        grid=(n_seq, nblk),
        in_specs=in_specs,
        out_specs=[row(v_w), out_state_spec],
        out_shape=[
            jax.ShapeDtypeStruct((t_total, v_w), out_dtype),
            jax.ShapeDtypeStruct((n_layers, n_seq, RET_HEADS, RET_DK, RET_DV), F32),
        ],
        input_output_aliases=aliases,
        compiler_params=_params("parallel", "arbitrary"),
        name="ret_core",
    )(*args)


def _ret_tables(chunk, chunk_pad):
    hds = jnp.arange(RET_HEADS, dtype=F32)
    log_gamma = jnp.log1p(-jnp.exp2(-5.0 - hds))
    idx = jnp.arange(chunk, dtype=F32)
    diff = idx[:, None] - idx[None, :]
    dmask = jnp.where(diff[None] >= 0,
                      jnp.exp(log_gamma[:, None, None] * jnp.maximum(diff, 0.0)[None]), 0.0)
    dmask = jnp.pad(dmask, ((0, 0), (0, 0), (0, chunk_pad - chunk)))
    q_dec = jnp.exp(log_gamma[:, None] * (idx[None, :] + 1.0))
    k_dec = jnp.exp(log_gamma[:, None] * (chunk - 1.0 - idx[None, :]))
    k_dec = jnp.pad(k_dec, ((0, 0), (0, chunk_pad - chunk)))
    c_dec = jnp.exp(log_gamma * chunk)
    qd = jnp.broadcast_to(q_dec[:, :, None], (RET_HEADS, chunk, RET_DK))
    kd = jnp.broadcast_to(k_dec[:, :, None], (RET_HEADS, chunk_pad, RET_DK))
    cd = jnp.broadcast_to(c_dec[:, None, None], (RET_HEADS, 1, RET_DV))
    return dmask.astype(F32), qd.astype(F32), kd.astype(F32), cd.astype(F32)


def _out_proj_body(a_ref, w_ref, x_ref, m_ref, o_ref):
    d = x_ref.shape[-1]
    y = _dot(a_ref[...].astype(BF16), w_ref[...])
    o_ref[...] = x_ref[...] + m_ref[:, 2 * d:3 * d] * y


def _out_proj_call(a, w_bf16, x, mod, blocks_per_seq):
    t_total, d = x.shape
    tm = min(TOKEN_BLOCK, t_total)
    ka = a.shape[1]
    return pl.pallas_call(
        _out_proj_body,
        grid=(t_total // tm,),
        in_specs=[
            pl.BlockSpec((tm, ka), lambda t: (t, 0)),
            _resident(w_bf16, 1),
            pl.BlockSpec((tm, d), lambda t: (t, 0)),
            _mod_spec(mod, tm, blocks_per_seq),
        ],
        out_specs=pl.BlockSpec((tm, d), lambda t: (t, 0)),
        out_shape=jax.ShapeDtypeStruct((t_total, d), F32),
        compiler_params=_params("parallel"),
        name="out_proj",
    )(a, w_bf16, x, mod)


def _kv_body(*refs, permuted):
    if permuted:
        (x_ref, g_ref, w_ref, kg_ref, ones_ref, perm_ref, slots_ref,
         k_ref, v_ref, kp_ref, vt_ref) = refs
    else:
        x_ref, g_ref, w_ref, kg_ref, ones_ref, k_ref, v_ref = refs
    x = x_ref[...]
    sb_w = SB_HEADS * SB_DH
    r = lax.rsqrt(jnp.mean(x * x, axis=-1, keepdims=True) + EPS)
    xn = (x * r * g_ref[...]).astype(BF16)
    k = _dot(xn, w_ref[:, :sb_w])
    v = _dot(xn, w_ref[:, sb_w:])
    ms = _head_sum64(k * k, ones_ref[...]) * (1.0 / SB_DH)
    k = k * lax.rsqrt(ms + EPS) * kg_ref[...]
    k_ref[...] = k
    v_ref[...] = v
    if permuted:
        kb = SB_KEY_BLOCK
        perm = perm_ref[...]
        cols = []
        for j in range(x.shape[0] // kb):
            rows = slice(j * kb, (j + 1) * kb)
            kp_ref[rows, :] = _expand_heads(_dot(perm, k[rows].astype(BF16)),
                                            slots_ref[...]).astype(BF16)
            cols.append(_dot(perm, v[rows].astype(BF16)).T.astype(BF16))
        vt_ref[0] = jnp.concatenate(cols, axis=1)


def _kv_call(x, g, w_bf16, k_gain_row, ones_bd, perm, one_slots, n_seq):
    t_total, d = x.shape
    tm = min(TOKEN_BLOCK, t_total)
    sb_w = SB_HEADS * SB_DH
    row = lambda w: pl.BlockSpec((tm, w), lambda t: (t, 0))
    const = lambda a: _resident(a, 1)
    in_specs = [row(d), pl.BlockSpec((1, d), lambda t: (0, 0)), const(w_bf16),
                const(k_gain_row), const(ones_bd)]
    args = [x, g.reshape(1, d), w_bf16, k_gain_row, ones_bd]
    out_specs = [row(sb_w), row(sb_w)]
    out_shape = [jax.ShapeDtypeStruct((t_total, sb_w), F32), jax.ShapeDtypeStruct((t_total, sb_w), F32)]
    if perm is not None:
        sup = SB_QUERY_BLOCK
        assert sup % tm == 0 and tm % SB_KEY_BLOCK == 0
        per_sup = sup // tm
        kp_w = SB_HEADS * LANES
        in_specs += [const(perm), const(one_slots)]
        args += [perm, one_slots]
        out_specs += [row(kp_w),
                      pl.BlockSpec((1, sb_w, tm), lambda t: (t // per_sup, 0, t % per_sup))]
        out_shape += [jax.ShapeDtypeStruct((t_total, kp_w), BF16),
                      jax.ShapeDtypeStruct((t_total // sup, sb_w, sup), BF16)]
    return pl.pallas_call(
        functools.partial(_kv_body, permuted=perm is not None),
        grid=(t_total // tm,),
        in_specs=in_specs,
        out_specs=out_specs,
        out_shape=out_shape,
        compiler_params=_params("parallel"),
        name="shared_kv",
    )(*args)


def _expand_heads(x, slots):
    lane = lax.broadcasted_iota(jnp.int32, (x.shape[0], LANES), 1)
    cols = []
    for hd in range(SB_HEADS):
        col = x[:, (hd // 2) * LANES:(hd // 2 + 1) * LANES]
        if hd % 2:
            col = pltpu.roll(col, SB_DH, axis=1)
        cols.append(jnp.where(lane < SB_DH, col, slots[:, hd * LANES:(hd + 1) * LANES]))
    return jnp.concatenate(cols, axis=1)


def _sb_proj_body(*refs, expand):
    if expand:
        x_ref, m_ref, g_ref, w_ref, qg_ref, ones_ref, slots_ref, q_ref, sg_ref = refs
    else:
        x_ref, m_ref, g_ref, w_ref, qg_ref, ones_ref, q_ref, sg_ref = refs
    h = _modulated(x_ref[...], g_ref[...], m_ref[...]).astype(BF16)
    sb_w = SB_HEADS * SB_DH
    q = _dot(h, w_ref[:, :sb_w])
    ms = _head_sum64(q * q, ones_ref[...]) * (1.0 / SB_DH)
    q = q * lax.rsqrt(ms + EPS) * qg_ref[...]
    if expand:
        q = _expand_heads(q, slots_ref[...])
    q_ref[...] = q.astype(q_ref.dtype)
    sg_ref[...] = _silu(_dot(h, w_ref[:, sb_w:])).astype(sg_ref.dtype)


def _sb_proj_call(x, mod, g, w_bf16, q_gain_row, ones_bd, bias_slots, blocks_per_seq, out_dtype):
    t_total, d = x.shape
    tm = min(TOKEN_BLOCK, t_total)
    sb_w = SB_HEADS * SB_DH
    row = lambda w: pl.BlockSpec((tm, w), lambda t: (t, 0))
    const = lambda a: _resident(a, 1)
    expand = bias_slots is not None
    q_w = SB_HEADS * LANES if expand else sb_w
    in_specs = [row(d), _mod_spec(mod, tm, blocks_per_seq), pl.BlockSpec((1, d), lambda t: (0, 0)),
                const(w_bf16), const(q_gain_row), const(ones_bd)]
    args = [x, mod, g.reshape(1, d), w_bf16, q_gain_row, ones_bd]
    if expand:
        in_specs.append(const(bias_slots))
        args.append(bias_slots)
    return pl.pallas_call(
        functools.partial(_sb_proj_body, expand=expand),
        grid=(t_total // tm,),
        in_specs=in_specs,
        out_specs=[row(q_w), row(sb_w)],
        out_shape=[jax.ShapeDtypeStruct((t_total, q_w), out_dtype),
                   jax.ShapeDtypeStruct((t_total, sb_w), out_dtype)],
        compiler_params=_params("parallel"),
        name="sb_proj",
    )(*args)


def _sb_beta_keep(u, mask):
    half_t = 0.5 * jnp.tanh(u)
    keep, beta = 0.5 - half_t, 0.5 + half_t
    if mask is not None:
        keep, beta = jnp.where(mask, keep, 1.0), jnp.where(mask, beta, 0.0)
    return beta, keep


def _sublane_suffix_products(x):
    sub = lax.broadcasted_iota(jnp.int32, x.shape, 0)
    inc = x
    for step in (1, 2, 4):
        shifted = pltpu.roll(inc, SUBLANES - step, axis=0)
        inc = inc * jnp.where(sub < SUBLANES - step, shifted, 1.0)
    after = jnp.where(sub < SUBLANES - 1, pltpu.roll(inc, SUBLANES - 1, axis=0), 1.0)
    return inc, after


def _sb_blocks(us, carry, masks):
    kb, nq = us[0].shape
    n = kb // SUBLANES
    runs = [None] * len(us)
    local = [[None] * n for _ in us]
    for i in range(n):
        rows = slice(i * SUBLANES, (i + 1) * SUBLANES)
        for b, (u, mask) in enumerate(zip(us, masks)):
            h_i = 0.5 * jnp.tanh(u[rows, :])
            if mask is not None:
                h_i = jnp.where(mask[rows, :], h_i, -0.5)
            beta_i, keep_i = 0.5 + h_i, 0.5 - h_i
            local[b][i] = beta_i if runs[b] is None else beta_i * runs[b]
            runs[b] = keep_i if runs[b] is None else runs[b] * keep_i
    ws = []
    for b in range(len(us)):
        inc, after = _sublane_suffix_products(runs[b])
        scale = carry * after
        carry = carry * jnp.broadcast_to(inc[0:1, :], (SUBLANES, nq))
        ws.append(jnp.concatenate([local[b][i] * scale for i in range(n)], axis=0))
    return ws, carry


def _sb_key_offsets(kb, nq):
    row = lax.broadcasted_iota(jnp.int32, (kb, nq), 0)
    n = kb // SUBLANES
    return (row & (SUBLANES - 1)) * n + (n - 1 - (row >> 3))


def _sb_prompt_body(q_ref, kp_ref, vt_ref, sg_ref, og_ref, acc_ref):
    qi = pl.program_id(2)
    qb = q_ref.shape[0]
    kb = SB_KEY_BLOCK
    ratio = qb // kb
    lg = SB_LANE_GROUP
    units = [(j, h) for j in range(2) for h in range(qb // lg)]
    q_unit = [q_ref[h * lg:(h + 1) * lg, j * LANES:(j + 1) * LANES] for j, h in units]
    order = list(range(ratio - 1, -1, -1))

    def logits(ks, u):
        r0 = pl.multiple_of(ks * qb, qb)
        j = units[u][0]
        return lax.dot_general(kp_ref[pl.ds(r0, qb), j * LANES:(j + 1) * LANES], q_unit[u],
                               (((1,), (1,)), ((), ())), preferred_element_type=F32)

    def weights(z_t, carry, masks):
        live = [c for c in order if masks is None or masks[c] is not None]
        ws, carry = _sb_blocks([z_t[c * kb:(c + 1) * kb, :] for c in live], carry,
                               [None if masks is None else masks[c] for c in live])
        by_block = dict(zip(live, ws))
        rows = [by_block[c] if c in by_block else jnp.zeros((kb, lg), F32) for c in range(ratio)]
        return jnp.concatenate(rows, axis=0).astype(BF16), carry

    def accumulate(ks, u, w):
        j, h = units[u]
        acc_ref[j, :, h * lg:(h + 1) * lg] += _dot(vt_ref[ks, j * SB_DH:(j + 1) * SB_DH, :], w)

    def super_block(ks, ks_next, z_first, w_last, carries, masks):
        z_t, w_prev = z_first, w_last
        out = []
        for u in range(len(units)):
            z_ahead = logits(ks, u + 1) if u + 1 < len(units) else logits(ks_next, 0)
            if u > 0:
                accumulate(ks, u - 1, w_prev)
            elif w_prev is not None:
                accumulate(ks + 1, len(units) - 1, w_prev)
            w_prev, carry = weights(z_t, carries[u], None if masks is None else masks[units[u][1]])
            out.append(carry)
            z_t = z_ahead
        return z_t, w_prev, tuple(out)

    acc_ref[...] = jnp.zeros_like(acc_ref)
    key_off = _sb_key_offsets(kb, lg)
    q_idx = lax.broadcasted_iota(jnp.int32, (kb, lg), 1)
    masks = [[key_off + c * kb < q_idx + h * lg if c * kb < (h + 1) * lg - 1 else None
              for c in range(ratio)] for h in range(qb // lg)]
    ones = jnp.ones((SUBLANES, lg), F32)
    state = super_block(qi, jnp.maximum(qi - 1, 0), logits(qi, 0), None, (ones,) * len(units), masks)

    def trips(first, count):
        def body(it, state):
            ks = first - count * it
            for c in range(count):
                state = super_block(ks - c, jnp.maximum(ks - c - 1, 0), *state, None)
            return state
        return body

    odd = qi & 1
    state = lax.fori_loop(0, odd, trips(qi - 1, 1), state)
    _, w_last, _ = lax.fori_loop(0, qi >> 1, trips(qi - 1 - odd, 2), state)
    accumulate(0, len(units) - 1, w_last)
    o = jnp.concatenate([acc_ref[0], acc_ref[1]], axis=0).T
    og_ref[...] = (o * sg_ref[...].astype(F32)).astype(og_ref.dtype)


def _sb_prompt_call(q, kp, vt, sg, n_seq):
    t_total, sb_w = sg.shape
    seq_len = t_total // n_seq
    qb = SB_QUERY_BLOCK
    assert seq_len % qb == 0
    n_q = seq_len // qb
    n_hp = sb_w // LANES
    return pl.pallas_call(
        _sb_prompt_body,
        grid=(n_seq, n_hp, n_q),
        in_specs=[
            pl.BlockSpec((qb, 2 * LANES), lambda b, hp, qi: (b * n_q + qi, hp)),
            pl.BlockSpec((seq_len, 2 * LANES), lambda b, hp, qi: (b, hp)),
            pl.BlockSpec((n_q, LANES, qb), lambda b, hp, qi: (b, hp, 0)),
            pl.BlockSpec((qb, LANES), lambda b, hp, qi: (b * n_q + qi, hp)),
        ],
        out_specs=pl.BlockSpec((qb, LANES), lambda b, hp, qi: (b * n_q + qi, hp)),
        scratch_shapes=[pltpu.VMEM((2, SB_DH, qb), F32)],
        out_shape=jax.ShapeDtypeStruct((t_total, sb_w), BF16),
        compiler_params=_params("parallel", "parallel", "arbitrary"),
        name="sb_attn_prompt",
    )(q, kp, vt, sg)


def _lane_suffix_products(x):
    lane = lax.broadcasted_iota(jnp.int32, x.shape, 1)
    inc = x
    step = 1
    while step < LANES:
        shifted = pltpu.roll(inc, LANES - step, axis=1)
        inc = inc * jnp.where(lane < LANES - step, shifted, 1.0)
        step *= 2
    after = jnp.where(lane < LANES - 1, pltpu.roll(inc, LANES - 1, axis=1), 1.0)
    return inc, after


def _sb_paged_body(pt_ref, q_ref, kn_ref, vn_ref, sg_ref, bias_ref, *rest, n_pages, n_new):
    kt_pages, vt_pages = rest[:n_pages], rest[n_pages:2 * n_pages]
    og_ref = rest[2 * n_pages]
    del pt_ref
    n_row = SB_HEADS * n_new
    pad_rows = PAGE_SIZE - n_new
    bias = bias_ref[...]

    def per_head(x):
        return jnp.stack([x[:, hd * SB_DH:(hd + 1) * SB_DH] for hd in range(SB_HEADS)], axis=0)

    q3 = per_head(q_ref[...])
    zeros_pad = jnp.zeros((pad_rows, SB_HEADS * SB_DH), F32)
    kn3 = per_head(jnp.concatenate([kn_ref[...], zeros_pad], axis=0))
    vn3 = per_head(jnp.concatenate([vn_ref[...], zeros_pad], axis=0))

    key_idx = lax.broadcasted_iota(jnp.int32, (n_row, PAGE_SIZE), 1)
    q_idx = lax.broadcasted_iota(jnp.int32, (n_row, PAGE_SIZE), 0) % n_new
    order = list(range(n_pages - 1, -1, -1))
    z_blocks = [lax.dot_general(q3, kn3, (((2,), (2,)), ((0,), (0,))), preferred_element_type=F32)]
    z_blocks += [lax.dot_general(q3, kt_pages[p][...], (((2,), (1,)), ((0,), (0,))),
                                 preferred_element_type=F32) for p in order]
    local, total = [], []
    for blk, z3 in enumerate(z_blocks):
        beta, keep = _sb_beta_keep(z3.reshape(n_row, PAGE_SIZE) + bias,
                                   key_idx < q_idx if blk == 0 else None)
        inc, after = _lane_suffix_products(keep)
        local.append(beta * after)
        total.append(jnp.broadcast_to(inc[:, 0:1], (n_row, PAGE_SIZE)))
    carry = None
    w_blocks = []
    for loc, tot in zip(local, total):
        w = loc if carry is None else loc * carry
        carry = tot if carry is None else carry * tot
        w_blocks.append(w.reshape(SB_HEADS, n_new, PAGE_SIZE))
    acc = lax.dot_general(w_blocks[0], vn3, (((2,), (1,)), ((0,), (0,))), preferred_element_type=F32)
    for w3, p in zip(w_blocks[1:], order):
        acc = acc + lax.dot_general(w3, vt_pages[p][...], (((2,), (2,)), ((0,), (0,))),
                                    preferred_element_type=F32)
    o = jnp.concatenate([acc[hd] for hd in range(SB_HEADS)], axis=1)
    og_ref[...] = o * sg_ref[...]


def _sb_paged_call(q, k_new, v_new, sg, bias_rows, kt_pool, vt_pool, page_table, n_new):
    t_total, sb_w = q.shape
    n_seq, n_pages = page_table.shape
    assert n_new == SUBLANES and SB_HEADS * n_new == PAGE_SIZE == LANES
    row = pl.BlockSpec((n_new, sb_w), lambda b, pt: (b, 0))

    def page_spec(p):
        return pl.BlockSpec((None, SB_HEADS, SB_DH, PAGE_SIZE), lambda b, pt: (pt[b, p], 0, 0, 0))

    grid_spec = pltpu.PrefetchScalarGridSpec(
        num_scalar_prefetch=1,
        grid=(n_seq,),
        in_specs=[row, row, row, row, pl.BlockSpec(bias_rows.shape, lambda b, pt: (0, 0))]
        + [page_spec(p) for p in range(n_pages)] * 2,
        out_specs=row,
    )
    return pl.pallas_call(
        functools.partial(_sb_paged_body, n_pages=n_pages, n_new=n_new),
        grid_spec=grid_spec,
        out_shape=jax.ShapeDtypeStruct((t_total, sb_w), F32),
        compiler_params=_params("arbitrary"),
        name="sb_attn_paged",
    )(page_table, q, k_new, v_new, sg, bias_rows, *([kt_pool] * n_pages), *([vt_pool] * n_pages))


def _rope_tables(pos):
    half = RET_DK // 2
    inv_freq = ROPE_BASE ** (-jnp.arange(half, dtype=F32) / half)
    ang = pos.astype(F32)[:, None] * inv_freq[None, :]
    cos, sin = jnp.cos(ang), jnp.sin(ang)
    return jnp.concatenate([cos, cos], axis=1), jnp.concatenate([-sin, sin], axis=1)


def _sb_perm_matrix(kb):
    row = jnp.arange(kb)
    n = kb // SUBLANES
    src = (row % SUBLANES) * n + (n - 1 - row // SUBLANES)
    return (src[:, None] == jnp.arange(kb)[None, :]).astype(BF16)


def _head_slots(vals):
    slots = jnp.zeros((SB_HEADS, LANES), F32).at[:, SB_DH:SB_DH + 2].set(vals)
    return slots.reshape(1, SB_HEADS * LANES)


def _bias_slots(half_bias):
    hi = half_bias.astype(BF16).astype(F32)
    lo = (half_bias - hi).astype(BF16).astype(F32)
    return _head_slots(jnp.stack([hi, lo], axis=1))


def _trunk(x, mods, blocks_per_seq, n_seq, rope, s0, attn_fn, w, act_dtype, permuted):
    (norm_g, ret_w_in, ret_norm_g, ret_w_out, kv_norm_g, w_kv, sb_q_g, sb_k_g,
     sb_w_in, sb_w_out, ones_bd, bias_slots) = w
    n_a = ret_w_in.shape[0]
    n_b = sb_w_in.shape[0]
    seq_len = x.shape[0] // n_seq
    chunk = math.gcd(seq_len, RET_CHUNK)
    tabs = _ret_tables(chunk, max(chunk, LANES))
    states = None
    for l in range(n_a):
        q, k, v, sg = _ret_proj_call(x, mods[l], norm_g[l], ret_w_in[l], rope[0], rope[1],
                                     blocks_per_seq, act_dtype)
        og, states = _ret_core_call(q, k, v, sg, tabs, ret_norm_g[l], s0, l, n_a, states,
                                    n_seq, act_dtype)
        x = _out_proj_call(og, ret_w_out[l], x, mods[l], blocks_per_seq)
    k_gain_row = jnp.tile(sb_k_g.astype(F32), SB_HEADS).reshape(1, -1)
    perm = _sb_perm_matrix(SB_KEY_BLOCK) if permuted else None
    kv = _kv_call(x, kv_norm_g, w_kv, k_gain_row, ones_bd, perm,
                  _head_slots(jnp.ones((SB_HEADS, 2), F32)) if permuted else None, n_seq)
    for j in range(n_b):
        l = n_a + j
        q_gain_row = jnp.tile(sb_q_g[j].astype(F32) * (0.5 * SB_DH ** -0.5), SB_HEADS).reshape(1, -1)
        q, sg = _sb_proj_call(x, mods[l], norm_g[l], sb_w_in[j], q_gain_row, ones_bd,
                              bias_slots[j] if permuted else None, blocks_per_seq, act_dtype)
        og = attn_fn(j, q, sg, kv)
        x = _out_proj_call(og, sb_w_out[j], x, mods[l], blocks_per_seq)
    return x, states, kv[0], kv[1]


def kernel(x_prompt, x_sample, state_ret, cache_k, cache_v, page_table, c_prompt, c_sample,
           ada_w, ada_b, norm_g, ret_w_in, ret_norm_g, ret_w_out,
           kv_norm_g, w_kv, sb_q_g, sb_k_g, sb_w_in, sb_w_out, sb_bias):
    n_p, len_p, d = x_prompt.shape
    n_s, len_s, _ = x_sample.shape
    depth = ada_w.shape[0]
    sb_w = SB_HEADS * SB_DH
    past_len = page_table.shape[1] * PAGE_SIZE

    n_c = n_p + n_s
    c_rows = -(-n_c // 16) * 16
    c_all = jnp.concatenate([c_prompt.astype(F32), c_sample.astype(F32),
                             jnp.zeros((c_rows - n_c, d), F32)], axis=0)
    mod = _ada_call(c_all, ada_w.astype(F32), ada_b.astype(F32))
    mods_p = [mod[l, :n_p].reshape(n_p, 1, 3 * d) for l in range(depth)]
    mods_s = [jnp.repeat(mod[l, n_p:n_c], len_s, axis=0).reshape(1, n_s * len_s, 3 * d)
              for l in range(depth)]

    ones_bd = (jnp.arange(sb_w)[:, None] // SB_DH == jnp.arange(sb_w)[None, :] // SB_DH).astype(BF16)
    half_bias = 0.5 * sb_bias.astype(F32)
    weights = (norm_g.astype(F32), ret_w_in.astype(BF16), ret_norm_g.astype(F32), ret_w_out.astype(BF16),
               kv_norm_g.astype(F32), w_kv.astype(BF16), sb_q_g, sb_k_g,
               sb_w_in.astype(BF16), sb_w_out.astype(BF16), ones_bd,
               [_bias_slots(half_bias[j]) for j in range(sb_bias.shape[0])])

    rope_p = _rope_tables(jnp.arange(len_p, dtype=jnp.int32))
    tm_p = min(TOKEN_BLOCK, len_p)

    def attn_p(j, q, sg, kv):
        return _sb_prompt_call(q, kv[2], kv[3], sg, n_p)

    y_p, st_p, k_p, v_p = _trunk(x_prompt.astype(F32).reshape(n_p * len_p, d), mods_p, len_p // tm_p,
                                 n_p, rope_p, None, attn_p, weights, BF16, True)

    pos_s = past_len + jnp.arange(len_s, dtype=jnp.int32)
    rope_s = tuple(jnp.tile(t, (n_s, 1)) for t in _rope_tables(pos_s))

    kt_pool = jnp.transpose(cache_k.astype(F32), (0, 2, 3, 1))
    vt_pool = jnp.transpose(cache_v.astype(F32), (0, 2, 3, 1))

    def attn_s(j, q, sg, kv):
        bias_rows = jnp.broadcast_to(jnp.repeat(half_bias[j], len_s)[:, None],
                                     (SB_HEADS * len_s, PAGE_SIZE))
        return _sb_paged_call(q, kv[0], kv[1], sg, bias_rows, kt_pool, vt_pool, page_table, len_s)

    y_s, st_s, k_s, v_s = _trunk(x_sample.astype(F32).reshape(n_s * len_s, d), mods_s, None,
                                 n_s, rope_s, state_ret.astype(F32), attn_s, weights, F32, False)

    return (y_p.reshape(n_p, len_p, d).astype(x_prompt.dtype),
            y_s.reshape(n_s, len_s, d).astype(x_sample.dtype),
            st_p, st_s,
            k_p.reshape(n_p, len_p, SB_HEADS, SB_DH), v_p.reshape(n_p, len_p, SB_HEADS, SB_DH),
            k_s.reshape(n_s, len_s, SB_HEADS, SB_DH), v_s.reshape(n_s, len_s, SB_HEADS, SB_DH))
```

```python
import functools
import math

import jax
import jax.numpy as jnp
from jax import lax
from jax.experimental import pallas as pl
from jax.experimental.pallas import tpu as pltpu

F32 = jnp.float32
BF16 = jnp.bfloat16

RET_HEADS = 8
RET_DK = 128
RET_DV = 256
RET_CHUNK = 128
SB_HEADS = 16
SB_DH = 64
PAGE_SIZE = 128
ROPE_BASE = 10000.0
EPS = 1e-6

LANES = 128
SUBLANES = 8
VMEM_LIMIT_BYTES = 56 * 1024 * 1024

TOKEN_BLOCK = 512
SB_KEY_BLOCK = 256
SB_QUERY_BLOCK = 512
SB_LANE_GROUP = 256
ADA_COL_BLOCK = 512
PAGED_GROUP = 6


def _params(*sem):
    return pltpu.CompilerParams(dimension_semantics=sem, vmem_limit_bytes=VMEM_LIMIT_BYTES)


def _resident(a, n_grid):
    zeros = (0,) * a.ndim
    index_map = {1: lambda t: zeros, 2: lambda b, t: zeros}[n_grid]
    return pl.BlockSpec(a.shape, index_map, pipeline_mode=pl.Buffered(1))


def _split_bf16(a):
    hi = a.astype(BF16)
    lo = (a - hi.astype(F32)).astype(BF16)
    return hi, lo


def _dot(a, b):
    return jnp.dot(a, b, preferred_element_type=F32)


def _dot3(a, b):
    a_hi, a_lo = _split_bf16(a)
    b_hi, b_lo = _split_bf16(b)
    return _dot(a_hi, b_hi) + _dot(a_lo, b_hi) + _dot(a_hi, b_lo)


def _silu(x):
    return x / (1.0 + jnp.exp(-x))


def _ada_body(c_ref, w_ref, b_ref, o_ref):
    o_ref[...] = _dot3(_silu(c_ref[...]), w_ref[...]) + b_ref[...]


def _ada_call(c_pad, ada_w, ada_b):
    depth, d, d3 = ada_w.shape
    rows = c_pad.shape[0]
    tn = ADA_COL_BLOCK
    return pl.pallas_call(
        _ada_body,
        grid=(depth, d3 // tn),
        in_specs=[
            pl.BlockSpec((rows, d), lambda l, j: (0, 0)),
            pl.BlockSpec((None, d, tn), lambda l, j: (l, 0, j)),
            pl.BlockSpec((None, 1, tn), lambda l, j: (l, 0, j)),
        ],
        out_specs=pl.BlockSpec((None, rows, tn), lambda l, j: (l, 0, j)),
        out_shape=jax.ShapeDtypeStruct((depth, rows, d3), F32),
        compiler_params=_params("parallel", "parallel"),
        name="ada_mod",
    )(c_pad, ada_w, ada_b.reshape(depth, 1, d3))


def _modulated(x, g, m):
    d = x.shape[-1]
    r = lax.rsqrt(jnp.mean(x * x, axis=-1, keepdims=True) + EPS)
    return x * r * g * (1.0 + m[:, d:2 * d]) + m[:, :d]


def _mod_spec(mod, tm, blocks_per_seq):
    d3 = mod.shape[-1]
    if mod.shape[1] == 1:
        return pl.BlockSpec((None, 1, d3), lambda t: (t // blocks_per_seq, 0, 0))
    return pl.BlockSpec((None, tm, d3), lambda t: (0, t, 0))


def _tab_spec(tab, tm, blocks_per_seq):
    if blocks_per_seq is None:
        return pl.BlockSpec((tm, LANES), lambda t: (t, 0))
    return pl.BlockSpec((tm, LANES), lambda t: (t % blocks_per_seq, 0))


def _head_sum64(x_sq, ones_bd):
    return _dot(x_sq.astype(BF16), ones_bd)


def _ret_proj_body(x_ref, m_ref, g_ref, w_ref, cos_ref, sin_ref, q_ref, k_ref, v_ref, sg_ref):
    h = _modulated(x_ref[...], g_ref[...], m_ref[...]).astype(BF16)
    cosf = cos_ref[...]
    sinf = sin_ref[...]
    qk_w = RET_HEADS * RET_DK
    v_w = RET_HEADS * RET_DV

    def rotary(p, scale):
        outs = []
        for hd in range(RET_HEADS):
            sl = p[:, hd * RET_DK:(hd + 1) * RET_DK]
            rot = pltpu.roll(sl, RET_DK // 2, axis=1)
            o = sl * cosf + rot * sinf
            outs.append(o if scale is None else o * scale)
        return jnp.concatenate(outs, axis=1)

    q = _dot(h, w_ref[:, 0:qk_w])
    q_ref[...] = rotary(q, None).astype(q_ref.dtype)
    k = _dot(h, w_ref[:, qk_w:2 * qk_w])
    k_ref[...] = rotary(k, RET_DK ** -0.5).astype(k_ref.dtype)
    v_ref[...] = _dot(h, w_ref[:, 2 * qk_w:2 * qk_w + v_w]).astype(v_ref.dtype)
    g = _dot(h, w_ref[:, 2 * qk_w + v_w:2 * qk_w + 2 * v_w])
    sg_ref[...] = _silu(g).astype(sg_ref.dtype)


def _ret_proj_call(x, mod, g, w_bf16, cosf, sinf, blocks_per_seq, out_dtype):
    t_total, d = x.shape
    tm = min(TOKEN_BLOCK, t_total)
    qk_w = RET_HEADS * RET_DK
    v_w = RET_HEADS * RET_DV
    row = lambda w: pl.BlockSpec((tm, w), lambda t: (t, 0))
    return pl.pallas_call(
        _ret_proj_body,
        grid=(t_total // tm,),
        in_specs=[
            row(d),
            _mod_spec(mod, tm, blocks_per_seq),
            pl.BlockSpec((1, d), lambda t: (0, 0)),
            _resident(w_bf16, 1),
            _tab_spec(cosf, tm, blocks_per_seq if mod.shape[1] == 1 else None),
            _tab_spec(sinf, tm, blocks_per_seq if mod.shape[1] == 1 else None),
        ],
        out_specs=[row(qk_w), row(qk_w), row(v_w), row(v_w)],
        out_shape=[
            jax.ShapeDtypeStruct((t_total, qk_w), out_dtype),
            jax.ShapeDtypeStruct((t_total, qk_w), out_dtype),
            jax.ShapeDtypeStruct((t_total, v_w), out_dtype),
            jax.ShapeDtypeStruct((t_total, v_w), out_dtype),
        ],
        compiler_params=_params("parallel"),
        name="ret_proj",
    )(x, mod, g.reshape(1, d), w_bf16, cosf, sinf)


def _ret_core_body(*refs, chunk, chunk_pad, n_chunks, has_s0, has_stack, layer):
    q_ref, k_ref, v_ref, sg_ref, dm_ref, qd_ref, kd_ref, cd_ref, ng_ref = refs[:9]
    s0_ref = refs[9] if has_s0 else None
    og_ref, stack_ref = refs[9 + has_s0 + has_stack:]
    mxu = BF16 if chunk >= 16 else F32
    s_ref = stack_ref if has_stack else stack_ref.at[layer]

    @pl.when(pl.program_id(1) == 0)
    def _():
        if not has_stack:
            stack_ref[...] = jnp.zeros_like(stack_ref)
        if has_s0:
            s_ref[...] = s0_ref[...]
        elif has_stack:
            s_ref[...] = jnp.zeros_like(s_ref)

    def pad_rows(a):
        if chunk_pad == chunk:
            return a
        return jnp.concatenate([a, jnp.zeros((chunk_pad - chunk, a.shape[1]), a.dtype)], axis=0)

    def one_chunk(ci, carry):
        r0 = pl.multiple_of(ci * chunk, chunk)
        rows = pl.ds(r0, chunk)
        for hd in range(RET_HEADS):
            qc = q_ref[rows, hd * RET_DK:(hd + 1) * RET_DK]
            kc = pad_rows(k_ref[rows, hd * RET_DK:(hd + 1) * RET_DK])
            vc = pad_rows(v_ref[rows, hd * RET_DV:(hd + 1) * RET_DV]).astype(mxu)
            s_old = s_ref[hd]
            scores = lax.dot_general(qc.astype(mxu), kc.astype(mxu), (((1,), (1,)), ((), ())),
                                     preferred_element_type=F32) * dm_ref[hd]
            intra = _dot(scores.astype(mxu), vc)
            qdec = (qc.astype(F32) * qd_ref[hd]).astype(mxu)
            cross = _dot(qdec, s_old.astype(mxu))
            o = intra + cross
            kdec = (kc.astype(F32) * kd_ref[hd]).astype(mxu)
            s_ref[hd] = s_old * cd_ref[hd] + lax.dot_general(
                kdec, vc, (((0,), (0,)), ((), ())), preferred_element_type=F32)
            r = lax.rsqrt(jnp.mean(o * o, axis=-1, keepdims=True) + EPS)
            cols = slice(hd * RET_DV, (hd + 1) * RET_DV)
            og = o * r * ng_ref[:, cols] * sg_ref[rows, cols].astype(F32)
            og_ref[rows, cols] = og.astype(og_ref.dtype)
        return carry

    if n_chunks == 1:
        one_chunk(0, 0)
    else:
        lax.fori_loop(0, n_chunks, one_chunk, 0)


def _ret_core_call(q, k, v, sg, tabs, norm_g, s0, layer, n_layers, states, n_seq, out_dtype):
    t_total = q.shape[0]
    seq_len = t_total // n_seq
    chunk = math.gcd(seq_len, RET_CHUNK)
    chunk_pad = max(chunk, LANES)
    tm = min(TOKEN_BLOCK, seq_len)
    n_chunks = tm // chunk
    nblk = seq_len // tm
    dm, qd, kd, cd = tabs
    qk_w = RET_HEADS * RET_DK
    v_w = RET_HEADS * RET_DV
    row = lambda w: pl.BlockSpec((tm, w), lambda b, t: (b * nblk + t, 0))
    full = lambda a: _resident(a, 2)
    state_spec = pl.BlockSpec((None, None, RET_HEADS, RET_DK, RET_DV),
                              lambda b, t: (layer, b, 0, 0, 0))
    in_specs = [row(qk_w), row(qk_w), row(v_w), row(v_w), full(dm), full(qd), full(kd), full(cd),
                pl.BlockSpec((1, v_w), lambda b, t: (0, 0))]
    args = [q, k, v, sg, dm, qd, kd, cd, norm_g.reshape(1, v_w)]
    if s0 is not None:
        in_specs.append(state_spec)
        args.append(s0)
    aliases = {}
    if states is not None:
        aliases = {len(args): 1}
        in_specs.append(pl.BlockSpec(memory_space=pl.ANY))
        args.append(states)
    body = functools.partial(_ret_core_body, chunk=chunk, chunk_pad=chunk_pad, n_chunks=n_chunks,
                             has_s0=s0 is not None, has_stack=states is not None, layer=layer)
    out_state_spec = state_spec if states is not None else pl.BlockSpec(
        (n_layers, None, RET_HEADS, RET_DK, RET_DV), lambda b, t: (0, b, 0, 0, 0))
    return pl.pallas_call(
        body,
        grid=(n_seq, nblk),
        in_specs=in_specs,
        out_specs=[row(v_w), out_state_spec],
        out_shape=[
            jax.ShapeDtypeStruct((t_total, v_w), out_dtype),
            jax.ShapeDtypeStruct((n_layers, n_seq, RET_HEADS, RET_DK, RET_DV), F32),
        ],
        input_output_aliases=aliases,
        compiler_params=_params("parallel", "arbitrary"),
        name="ret_core",
    )(*args)


def _ret_tables(chunk, chunk_pad):
    hds = jnp.arange(RET_HEADS, dtype=F32)
    log_gamma = jnp.log1p(-jnp.exp2(-5.0 - hds))
    idx = jnp.arange(chunk, dtype=F32)
    diff = idx[:, None] - idx[None, :]
    dmask = jnp.where(diff[None] >= 0,
                      jnp.exp(log_gamma[:, None, None] * jnp.maximum(diff, 0.0)[None]), 0.0)
    dmask = jnp.pad(dmask, ((0, 0), (0, 0), (0, chunk_pad - chunk)))
    q_dec = jnp.exp(log_gamma[:, None] * (idx[None, :] + 1.0))
    k_dec = jnp.exp(log_gamma[:, None] * (chunk - 1.0 - idx[None, :]))
    k_dec = jnp.pad(k_dec, ((0, 0), (0, chunk_pad - chunk)))
    c_dec = jnp.exp(log_gamma * chunk)
    qd = jnp.broadcast_to(q_dec[:, :, None], (RET_HEADS, chunk, RET_DK))
    kd = jnp.broadcast_to(k_dec[:, :, None], (RET_HEADS, chunk_pad, RET_DK))
    cd = jnp.broadcast_to(c_dec[:, None, None], (RET_HEADS, 1, RET_DV))
    return dmask.astype(F32), qd.astype(F32), kd.astype(F32), cd.astype(F32)


def _out_proj_body(a_ref, w_ref, x_ref, m_ref, o_ref):
    d = x_ref.shape[-1]
    y = _dot(a_ref[...].astype(BF16), w_ref[...])
    o_ref[...] = x_ref[...] + m_ref[:, 2 * d:3 * d] * y


def _out_proj_call(a, w_bf16, x, mod, blocks_per_seq):
    t_total, d = x.shape
    tm = min(TOKEN_BLOCK, t_total)
    ka = a.shape[1]
    return pl.pallas_call(
        _out_proj_body,
        grid=(t_total // tm,),
        in_specs=[
            pl.BlockSpec((tm, ka), lambda t: (t, 0)),
            _resident(w_bf16, 1),
            pl.BlockSpec((tm, d), lambda t: (t, 0)),
            _mod_spec(mod, tm, blocks_per_seq),
        ],
        out_specs=pl.BlockSpec((tm, d), lambda t: (t, 0)),
        out_shape=jax.ShapeDtypeStruct((t_total, d), F32),
        compiler_params=_params("parallel"),
        name="out_proj",
    )(a, w_bf16, x, mod)


def _kv_body(*refs, permuted):
    if permuted:
        (x_ref, g_ref, w_ref, kg_ref, ones_ref, perm_ref, slots_ref,
         k_ref, v_ref, kp_ref, vt_ref) = refs
    else:
        x_ref, g_ref, w_ref, kg_ref, ones_ref, k_ref, v_ref = refs
    x = x_ref[...]
    sb_w = SB_HEADS * SB_DH
    r = lax.rsqrt(jnp.mean(x * x, axis=-1, keepdims=True) + EPS)
    xn = (x * r * g_ref[...]).astype(BF16)
    k = _dot(xn, w_ref[:, :sb_w])
    v = _dot(xn, w_ref[:, sb_w:])
    ms = _head_sum64(k * k, ones_ref[...]) * (1.0 / SB_DH)
    k = k * lax.rsqrt(ms + EPS) * kg_ref[...]
    k_ref[...] = k
    v_ref[...] = v
    if permuted:
        kb = SB_KEY_BLOCK
        perm = perm_ref[...]
        cols = []
        for j in range(x.shape[0] // kb):
            rows = slice(j * kb, (j + 1) * kb)
            kp_ref[rows, :] = _expand_heads(_dot(perm, k[rows].astype(BF16)),
                                            slots_ref[...]).astype(BF16)
            cols.append(_dot(perm, v[rows].astype(BF16)).T.astype(BF16))
        vt_ref[0] = jnp.concatenate(cols, axis=1)


def _kv_call(x, g, w_bf16, k_gain_row, ones_bd, perm, one_slots, n_seq):
    t_total, d = x.shape
    tm = min(TOKEN_BLOCK, t_total)
    sb_w = SB_HEADS * SB_DH
    row = lambda w: pl.BlockSpec((tm, w), lambda t: (t, 0))
    const = lambda a: _resident(a, 1)
    in_specs = [row(d), pl.BlockSpec((1, d), lambda t: (0, 0)), const(w_bf16),
                const(k_gain_row), const(ones_bd)]
    args = [x, g.reshape(1, d), w_bf16, k_gain_row, ones_bd]
    out_specs = [row(sb_w), row(sb_w)]
    out_shape = [jax.ShapeDtypeStruct((t_total, sb_w), F32), jax.ShapeDtypeStruct((t_total, sb_w), F32)]
    if perm is not None:
        sup = SB_QUERY_BLOCK
        assert sup % tm == 0 and tm % SB_KEY_BLOCK == 0
        per_sup = sup // tm
        kp_w = SB_HEADS * LANES
        in_specs += [const(perm), const(one_slots)]
        args += [perm, one_slots]
        out_specs += [row(kp_w),
                      pl.BlockSpec((1, sb_w, tm), lambda t: (t // per_sup, 0, t % per_sup))]
        out_shape += [jax.ShapeDtypeStruct((t_total, kp_w), BF16),
                      jax.ShapeDtypeStruct((t_total // sup, sb_w, sup), BF16)]
    return pl.pallas_call(
        functools.partial(_kv_body, permuted=perm is not None),
        grid=(t_total // tm,),
        in_specs=in_specs,
        out_specs=out_specs,
        out_shape=out_shape,
        compiler_params=_params("parallel"),
        name="shared_kv",
    )(*args)


def _expand_heads(x, slots):
    lane = lax.broadcasted_iota(jnp.int32, (x.shape[0], LANES), 1)
    cols = []
    for hd in range(SB_HEADS):
        col = x[:, (hd // 2) * LANES:(hd // 2 + 1) * LANES]
        if hd % 2:
            col = pltpu.roll(col, SB_DH, axis=1)
        cols.append(jnp.where(lane < SB_DH, col, slots[:, hd * LANES:(hd + 1) * LANES]))
    return jnp.concatenate(cols, axis=1)


def _sb_proj_body(*refs, expand):
    if expand:
        x_ref, m_ref, g_ref, w_ref, qg_ref, ones_ref, slots_ref, q_ref, sg_ref = refs
    else:
        x_ref, m_ref, g_ref, w_ref, qg_ref, ones_ref, q_ref, sg_ref = refs
    h = _modulated(x_ref[...], g_ref[...], m_ref[...]).astype(BF16)
    sb_w = SB_HEADS * SB_DH
    q = _dot(h, w_ref[:, :sb_w])
    ms = _head_sum64(q * q, ones_ref[...]) * (1.0 / SB_DH)
    q = q * lax.rsqrt(ms + EPS) * qg_ref[...]
    if expand:
        q = _expand_heads(q, slots_ref[...])
    q_ref[...] = q.astype(q_ref.dtype)
    sg_ref[...] = _silu(_dot(h, w_ref[:, sb_w:])).astype(sg_ref.dtype)


def _sb_proj_call(x, mod, g, w_bf16, q_gain_row, ones_bd, bias_slots, blocks_per_seq, out_dtype):
    t_total, d = x.shape
    tm = min(TOKEN_BLOCK, t_total)
    sb_w = SB_HEADS * SB_DH
    row = lambda w: pl.BlockSpec((tm, w), lambda t: (t, 0))
    const = lambda a: _resident(a, 1)
    expand = bias_slots is not None
    q_w = SB_HEADS * LANES if expand else sb_w
    in_specs = [row(d), _mod_spec(mod, tm, blocks_per_seq), pl.BlockSpec((1, d), lambda t: (0, 0)),
                const(w_bf16), const(q_gain_row), const(ones_bd)]
    args = [x, mod, g.reshape(1, d), w_bf16, q_gain_row, ones_bd]
    if expand:
        in_specs.append(const(bias_slots))
        args.append(bias_slots)
    return pl.pallas_call(
        functools.partial(_sb_proj_body, expand=expand),
        grid=(t_total // tm,),
        in_specs=in_specs,
        out_specs=[row(q_w), row(sb_w)],
        out_shape=[jax.ShapeDtypeStruct((t_total, q_w), out_dtype),
                   jax.ShapeDtypeStruct((t_total, sb_w), out_dtype)],
        compiler_params=_params("parallel"),
        name="sb_proj",
    )(*args)


def _sb_beta_keep(u, mask):
    half_t = 0.5 * jnp.tanh(u)
    keep, beta = 0.5 - half_t, 0.5 + half_t
    if mask is not None:
        keep, beta = jnp.where(mask, keep, 1.0), jnp.where(mask, beta, 0.0)
    return beta, keep


def _sublane_suffix_products(x):
    sub = lax.broadcasted_iota(jnp.int32, x.shape, 0)
    inc = x
    for step in (1, 2, 4):
        shifted = pltpu.roll(inc, SUBLANES - step, axis=0)
        inc = inc * jnp.where(sub < SUBLANES - step, shifted, 1.0)
    after = jnp.where(sub < SUBLANES - 1, pltpu.roll(inc, SUBLANES - 1, axis=0), 1.0)
    return inc, after


def _sb_blocks(us, carry, masks):
    kb, nq = us[0].shape
    n = kb // SUBLANES
    runs = [None] * len(us)
    local = [[None] * n for _ in us]
    for i in range(n):
        rows = slice(i * SUBLANES, (i + 1) * SUBLANES)
        for b, (u, mask) in enumerate(zip(us, masks)):
            h_i = 0.5 * jnp.tanh(u[rows, :])
            if mask is not None:
                h_i = jnp.where(mask[rows, :], h_i, -0.5)
            beta_i, keep_i = 0.5 + h_i, 0.5 - h_i
            local[b][i] = beta_i if runs[b] is None else beta_i * runs[b]
            runs[b] = keep_i if runs[b] is None else runs[b] * keep_i
    ws = []
    for b in range(len(us)):
        inc, after = _sublane_suffix_products(runs[b])
        scale = carry * after
        carry = carry * jnp.broadcast_to(inc[0:1, :], (SUBLANES, nq))
        ws.append(jnp.concatenate([local[b][i] * scale for i in range(n)], axis=0))
    return ws, carry


def _sb_key_offsets(kb, nq):
    row = lax.broadcasted_iota(jnp.int32, (kb, nq), 0)
    n = kb // SUBLANES
    return (row & (SUBLANES - 1)) * n + (n - 1 - (row >> 3))


def _sb_prompt_body(q_ref, kp_ref, vt_ref, sg_ref, og_ref, acc_ref):
    qi = pl.program_id(2)
    qb = q_ref.shape[0]
    kb = SB_KEY_BLOCK
    ratio = qb // kb
    lg = SB_LANE_GROUP
    units = [(j, h) for j in range(2) for h in range(qb // lg)]
    q_unit = [q_ref[h * lg:(h + 1) * lg, j * LANES:(j + 1) * LANES] for j, h in units]
    order = list(range(ratio - 1, -1, -1))

    def logits(ks, u):
        r0 = pl.multiple_of(ks * qb, qb)
        j = units[u][0]
        return lax.dot_general(kp_ref[pl.ds(r0, qb), j * LANES:(j + 1) * LANES], q_unit[u],
                               (((1,), (1,)), ((), ())), preferred_element_type=F32)

    def weights(z_t, carry, masks):
        live = [c for c in order if masks is None or masks[c] is not None]
        ws, carry = _sb_blocks([z_t[c * kb:(c + 1) * kb, :] for c in live], carry,
                               [None if masks is None else masks[c] for c in live])
        by_block = dict(zip(live, ws))
        rows = [by_block[c] if c in by_block else jnp.zeros((kb, lg), F32) for c in range(ratio)]
        return jnp.concatenate(rows, axis=0).astype(BF16), carry

    def accumulate(ks, u, w):
        j, h = units[u]
        acc_ref[j, :, h * lg:(h + 1) * lg] += _dot(vt_ref[ks, j * SB_DH:(j + 1) * SB_DH, :], w)

    def super_block(ks, ks_next, z_first, w_last, carries, masks):
        z_t, w_prev = z_first, w_last
        out = []
        for u in range(len(units)):
            z_ahead = logits(ks, u + 1) if u + 1 < len(units) else logits(ks_next, 0)
            if u > 0:
                accumulate(ks, u - 1, w_prev)
            elif w_prev is not None:
                accumulate(ks + 1, len(units) - 1, w_prev)
            w_prev, carry = weights(z_t, carries[u], None if masks is None else masks[units[u][1]])
            out.append(carry)
            z_t = z_ahead
        return z_t, w_prev, tuple(out)

    acc_ref[...] = jnp.zeros_like(acc_ref)
    key_off = _sb_key_offsets(kb, lg)
    q_idx = lax.broadcasted_iota(jnp.int32, (kb, lg), 1)
    masks = [[key_off + c * kb < q_idx + h * lg if c * kb < (h + 1) * lg - 1 else None
              for c in range(ratio)] for h in range(qb // lg)]
    ones = jnp.ones((SUBLANES, lg), F32)
    state = super_block(qi, jnp.maximum(qi - 1, 0), logits(qi, 0), None, (ones,) * len(units), masks)

    def trips(first, count):
        def body(it, state):
            ks = first - count * it
            for c in range(count):
                state = super_block(ks - c, jnp.maximum(ks - c - 1, 0), *state, None)
            return state
        return body

    odd = qi & 1
    state = lax.fori_loop(0, odd, trips(qi - 1, 1), state)
    _, w_last, _ = lax.fori_loop(0, qi >> 1, trips(qi - 1 - odd, 2), state)
    accumulate(0, len(units) - 1, w_last)
    o = jnp.concatenate([acc_ref[0], acc_ref[1]], axis=0).T
    og_ref[...] = (o * sg_ref[...].astype(F32)).astype(og_ref.dtype)


def _sb_prompt_call(q, kp, vt, sg, n_seq):
    t_total, sb_w = sg.shape
    seq_len = t_total // n_seq
    qb = SB_QUERY_BLOCK
    assert seq_len % qb == 0
    n_q = seq_len // qb
    n_hp = sb_w // LANES
    return pl.pallas_call(
        _sb_prompt_body,
        grid=(n_seq, n_hp, n_q),
        in_specs=[
            pl.BlockSpec((qb, 2 * LANES), lambda b, hp, qi: (b * n_q + qi, hp)),
            pl.BlockSpec((seq_len, 2 * LANES), lambda b, hp, qi: (b, hp)),
            pl.BlockSpec((n_q, LANES, qb), lambda b, hp, qi: (b, hp, 0)),
            pl.BlockSpec((qb, LANES), lambda b, hp, qi: (b * n_q + qi, hp)),
        ],
        out_specs=pl.BlockSpec((qb, LANES), lambda b, hp, qi: (b * n_q + qi, hp)),
        scratch_shapes=[pltpu.VMEM((2, SB_DH, qb), F32)],
        out_shape=jax.ShapeDtypeStruct((t_total, sb_w), BF16),
        compiler_params=_params("parallel", "parallel", "arbitrary"),
        name="sb_attn_prompt",
    )(q, kp, vt, sg)


def _sb_paged_body(pt_ref, q_ref, kn_ref, vn_ref, sg_ref, bias_ref, *rest, n_pages, n_new):
    kt_pages, vt_pages = rest[:n_pages], rest[n_pages:2 * n_pages]
    og_ref = rest[2 * n_pages]
    del pt_ref
    n_row = SB_HEADS * n_new
    pad_rows = PAGE_SIZE - n_new
    bias = bias_ref[...]

    def per_head(x):
        return jnp.stack([x[:, hd * SB_DH:(hd + 1) * SB_DH] for hd in range(SB_HEADS)], axis=0)

    q3 = per_head(q_ref[...])
    zeros_pad = jnp.zeros((pad_rows, SB_HEADS * SB_DH), F32)
    kn3 = per_head(jnp.concatenate([kn_ref[...], zeros_pad], axis=0))
    vn3 = per_head(jnp.concatenate([vn_ref[...], zeros_pad], axis=0))

    key_idx = lax.broadcasted_iota(jnp.int32, (PAGE_SIZE, n_row), 0)
    q_idx = lax.broadcasted_iota(jnp.int32, (PAGE_SIZE, n_row), 1) % n_new
    blocks = [None] + list(range(n_pages - 1, -1, -1))
    groups = [blocks[i:i + PAGED_GROUP] for i in range(0, len(blocks), PAGED_GROUP)]

    def logits(group):
        return [lax.dot_general(q3, kn3, (((2,), (2,)), ((0,), (0,))), preferred_element_type=F32)
                if p is None else
                lax.dot_general(q3, kt_pages[p][...], (((2,), (1,)), ((0,), (0,))),
                                preferred_element_type=F32) for p in group]

    carry, acc = None, None
    z_ahead = logits(groups[0])
    for g, group in enumerate(groups):
        z_group = z_ahead
        if g + 1 < len(groups):
            z_ahead = logits(groups[g + 1])
        local, total = [], []
        for p, z3 in zip(group, z_group):
            z_t = (z3.reshape(n_row, PAGE_SIZE) + bias).T
            mask_t = key_idx < q_idx if p is None else None
            loc_rows, tot_rows = [], []
            for i in range(PAGE_SIZE // SUBLANES):
                rows = slice(i * SUBLANES, (i + 1) * SUBLANES)
                beta, keep = _sb_beta_keep(z_t[rows, :], None if mask_t is None else mask_t[rows, :])
                inc, after = _sublane_suffix_products(keep)
                loc_rows.append(beta * after)
                tot_rows.append(jnp.broadcast_to(inc[0:1, :], (SUBLANES, n_row)))
            local.append(loc_rows)
            total.append(tot_rows)
        for p, loc_rows, tot_rows in zip(group, local, total):
            w_rows = [None] * len(loc_rows)
            for i in range(len(loc_rows) - 1, -1, -1):
                w_rows[i] = loc_rows[i] if carry is None else loc_rows[i] * carry
                carry = tot_rows[i] if carry is None else carry * tot_rows[i]
            w3 = jnp.concatenate(w_rows, axis=0).T.reshape(SB_HEADS, n_new, PAGE_SIZE)
            if p is None:
                out = lax.dot_general(w3, vn3, (((2,), (1,)), ((0,), (0,))), preferred_element_type=F32)
            else:
                out = lax.dot_general(w3, vt_pages[p][...], (((2,), (2,)), ((0,), (0,))),
                                      preferred_element_type=F32)
            acc = out if acc is None else acc + out
    o = jnp.concatenate([acc[hd] for hd in range(SB_HEADS)], axis=1)
    og_ref[...] = o * sg_ref[...]


def _sb_paged_call(q, k_new, v_new, sg, bias_rows, kt_pool, vt_pool, page_table, n_new):
    t_total, sb_w = q.shape
    n_seq, n_pages = page_table.shape
    assert n_new == SUBLANES and SB_HEADS * n_new == PAGE_SIZE == LANES
    row = pl.BlockSpec((n_new, sb_w), lambda b, pt: (b, 0))

    def page_spec(p):
        return pl.BlockSpec((None, SB_HEADS, SB_DH, PAGE_SIZE), lambda b, pt: (pt[b, p], 0, 0, 0))

    grid_spec = pltpu.PrefetchScalarGridSpec(
        num_scalar_prefetch=1,
        grid=(n_seq,),
        in_specs=[row, row, row, row, pl.BlockSpec(bias_rows.shape, lambda b, pt: (0, 0))]
        + [page_spec(p) for p in range(n_pages)] * 2,
        out_specs=row,
    )
    return pl.pallas_call(
        functools.partial(_sb_paged_body, n_pages=n_pages, n_new=n_new),
        grid_spec=grid_spec,
        out_shape=jax.ShapeDtypeStruct((t_total, sb_w), F32),
        compiler_params=_params("arbitrary"),
        name="sb_attn_paged",
    )(page_table, q, k_new, v_new, sg, bias_rows, *([kt_pool] * n_pages), *([vt_pool] * n_pages))


def _rope_tables(pos):
    half = RET_DK // 2
    inv_freq = ROPE_BASE ** (-jnp.arange(half, dtype=F32) / half)
    ang = pos.astype(F32)[:, None] * inv_freq[None, :]
    cos, sin = jnp.cos(ang), jnp.sin(ang)
    return jnp.concatenate([cos, cos], axis=1), jnp.concatenate([-sin, sin], axis=1)


def _sb_perm_matrix(kb):
    row = jnp.arange(kb)
    n = kb // SUBLANES
    src = (row % SUBLANES) * n + (n - 1 - row // SUBLANES)
    return (src[:, None] == jnp.arange(kb)[None, :]).astype(BF16)


def _head_slots(vals):
    slots = jnp.zeros((SB_HEADS, LANES), F32).at[:, SB_DH:SB_DH + 2].set(vals)
    return slots.reshape(1, SB_HEADS * LANES)


def _bias_slots(half_bias):
    hi = half_bias.astype(BF16).astype(F32)
    lo = (half_bias - hi).astype(BF16).astype(F32)
    return _head_slots(jnp.stack([hi, lo], axis=1))


def _trunk(x, mods, blocks_per_seq, n_seq, rope, s0, attn_fn, w, act_dtype, permuted):
    (norm_g, ret_w_in, ret_norm_g, ret_w_out, kv_norm_g, w_kv, sb_q_g, sb_k_g,
     sb_w_in, sb_w_out, ones_bd, bias_slots) = w
    n_a = ret_w_in.shape[0]
    n_b = sb_w_in.shape[0]
    seq_len = x.shape[0] // n_seq
    chunk = math.gcd(seq_len, RET_CHUNK)
    tabs = _ret_tables(chunk, max(chunk, LANES))
    states = None
    for l in range(n_a):
        q, k, v, sg = _ret_proj_call(x, mods[l], norm_g[l], ret_w_in[l], rope[0], rope[1],
                                     blocks_per_seq, act_dtype)
        og, states = _ret_core_call(q, k, v, sg, tabs, ret_norm_g[l], s0, l, n_a, states,
                                    n_seq, act_dtype)
        x = _out_proj_call(og, ret_w_out[l], x, mods[l], blocks_per_seq)
    k_gain_row = jnp.tile(sb_k_g.astype(F32), SB_HEADS).reshape(1, -1)
    perm = _sb_perm_matrix(SB_KEY_BLOCK) if permuted else None
    kv = _kv_call(x, kv_norm_g, w_kv, k_gain_row, ones_bd, perm,
                  _head_slots(jnp.ones((SB_HEADS, 2), F32)) if permuted else None, n_seq)
    for j in range(n_b):
        l = n_a + j
        q_gain_row = jnp.tile(sb_q_g[j].astype(F32) * (0.5 * SB_DH ** -0.5), SB_HEADS).reshape(1, -1)
        q, sg = _sb_proj_call(x, mods[l], norm_g[l], sb_w_in[j], q_gain_row, ones_bd,
                              bias_slots[j] if permuted else None, blocks_per_seq, act_dtype)
        og = attn_fn(j, q, sg, kv)
        x = _out_proj_call(og, sb_w_out[j], x, mods[l], blocks_per_seq)
    return x, states, kv[0], kv[1]


def kernel(x_prompt, x_sample, state_ret, cache_k, cache_v, page_table, c_prompt, c_sample,
           ada_w, ada_b, norm_g, ret_w_in, ret_norm_g, ret_w_out,
           kv_norm_g, w_kv, sb_q_g, sb_k_g, sb_w_in, sb_w_out, sb_bias):
    n_p, len_p, d = x_prompt.shape
    n_s, len_s, _ = x_sample.shape
    depth = ada_w.shape[0]
    sb_w = SB_HEADS * SB_DH
    past_len = page_table.shape[1] * PAGE_SIZE

    n_c = n_p + n_s
    c_rows = -(-n_c // 16) * 16
    c_all = jnp.concatenate([c_prompt.astype(F32), c_sample.astype(F32),
                             jnp.zeros((c_rows - n_c, d), F32)], axis=0)
    mod = _ada_call(c_all, ada_w.astype(F32), ada_b.astype(F32))
    mods_p = [mod[l, :n_p].reshape(n_p, 1, 3 * d) for l in range(depth)]
    mods_s = [jnp.repeat(mod[l, n_p:n_c], len_s, axis=0).reshape(1, n_s * len_s, 3 * d)
              for l in range(depth)]

    ones_bd = (jnp.arange(sb_w)[:, None] // SB_DH == jnp.arange(sb_w)[None, :] // SB_DH).astype(BF16)
    half_bias = 0.5 * sb_bias.astype(F32)
    weights = (norm_g.astype(F32), ret_w_in.astype(BF16), ret_norm_g.astype(F32), ret_w_out.astype(BF16),
               kv_norm_g.astype(F32), w_kv.astype(BF16), sb_q_g, sb_k_g,
               sb_w_in.astype(BF16), sb_w_out.astype(BF16), ones_bd,
               [_bias_slots(half_bias[j]) for j in range(sb_bias.shape[0])])

    rope_p = _rope_tables(jnp.arange(len_p, dtype=jnp.int32))
    tm_p = min(TOKEN_BLOCK, len_p)

    def attn_p(j, q, sg, kv):
        return _sb_prompt_call(q, kv[2], kv[3], sg, n_p)

    y_p, st_p, k_p, v_p = _trunk(x_prompt.astype(F32).reshape(n_p * len_p, d), mods_p, len_p // tm_p,
                                 n_p, rope_p, None, attn_p, weights, BF16, True)

    pos_s = past_len + jnp.arange(len_s, dtype=jnp.int32)
    rope_s = tuple(jnp.tile(t, (n_s, 1)) for t in _rope_tables(pos_s))

    kt_pool = jnp.transpose(cache_k.astype(F32), (0, 2, 3, 1))
    vt_pool = jnp.transpose(cache_v.astype(F32), (0, 2, 3, 1))

    def attn_s(j, q, sg, kv):
        bias_rows = jnp.broadcast_to(jnp.repeat(half_bias[j], len_s)[:, None],
                                     (SB_HEADS * len_s, PAGE_SIZE))
        return _sb_paged_call(q, kv[0], kv[1], sg, bias_rows, kt_pool, vt_pool, page_table, len_s)

    y_s, st_s, k_s, v_s = _trunk(x_sample.astype(F32).reshape(n_s * len_s, d), mods_s, None,
                                 n_s, rope_s, state_ret.astype(F32), attn_s, weights, F32, False)

    return (y_p.reshape(n_p, len_p, d).astype(x_prompt.dtype),
            y_s.reshape(n_s, len_s, d).astype(x_sample.dtype),
            st_p, st_s,
            k_p.reshape(n_p, len_p, SB_HEADS, SB_DH), v_p.reshape(n_p, len_p, SB_HEADS, SB_DH),
            k_s.reshape(n_s, len_s, SB_HEADS, SB_DH), v_s.reshape(n_s, len_s, SB_HEADS, SB_DH))
```
